```python
import math
import jax, jax.numpy as jnp
from jax import lax
import numpy as np

D_MODEL = 1024
BATCH = 8
SEQ = 4096
DEPTH = 1
DEC_BATCH = 1
DEC_SEQ = 16384
PAST_LEN = 128

HEAD_DIM = 64
GRID_W = 64
NA_HEADS = 8
NA_ROWS = 8
NA_COLS = 16
NA_QBLK = 16
NA_KBLK = 32
NA_WIDTH = NA_HEADS * HEAD_DIM
DIL_GROUPS = ((128, 1), (512, 4), (2048, 16))
DIL_HEADS = 8
DIL_WIDTH = DIL_HEADS * HEAD_DIM
ROPE_THETA = 10000.0
EPS = 1e-6
NEG = -1e30
IN_SIZES = ((NA_WIDTH,) * 4
            + (DIL_WIDTH,) * (3 * len(DIL_GROUPS))
            + (DIL_WIDTH,)
            + (D_MODEL, D_MODEL))
D_IN = sum(IN_SIZES)

kernel_name = "hybrid_natten_dilated_encoder"


def _split_points(sizes):
    pts, acc = [], 0
    for s in sizes[:-1]:
        acc += s
        pts.append(acc)
    return pts


def rms_norm(x, g):
    xf = x.astype(jnp.float32)
    y = xf * lax.rsqrt(jnp.mean(xf * xf, axis=-1, keepdims=True) + EPS)
    return (y * g.astype(jnp.float32)).astype(x.dtype)


def rope(x):
    L, dh = x.shape[1], x.shape[-1]
    inv = ROPE_THETA ** (-jnp.arange(0, dh, 2, dtype=jnp.float32) / dh)
    ang = jnp.arange(L, dtype=jnp.float32)[:, None] * inv[None, :]
    cos = jnp.cos(ang)[None, :, None, :]
    sin = jnp.sin(ang)[None, :, None, :]
    xf = x.astype(jnp.float32)
    x1, x2 = xf[..., : dh // 2], xf[..., dh // 2:]
    return jnp.concatenate([x1 * cos - x2 * sin, x2 * cos + x1 * sin], axis=-1).astype(x.dtype)


def neighborhood_attention(q, k, v, rel_bias):
    B, L, H, dh = q.shape
    rows = L // GRID_W
    kr = min(NA_ROWS, rows)
    ncb = GRID_W // NA_QBLK
    r = jnp.arange(rows)
    row_start = jnp.clip(r - kr // 2, 0, rows - kr)
    key_rows = row_start[:, None] + jnp.arange(kr)[None, :]
    qcol = jnp.arange(ncb)[:, None] * NA_QBLK + jnp.arange(NA_QBLK)[None, :]
    kblk_start = jnp.clip(jnp.arange(ncb) * NA_QBLK - NA_COLS // 2, 0, GRID_W - NA_KBLK)
    key_cols = kblk_start[:, None] + jnp.arange(NA_KBLK)[None, :]
    cstart = jnp.clip(qcol - NA_COLS // 2, 0, GRID_W - NA_COLS)
    kc = key_cols[:, None, :]
    col_ok = (kc >= cstart[:, :, None]) & (kc < cstart[:, :, None] + NA_COLS)
    dcol = jnp.clip(kc - qcol[:, :, None], -(NA_COLS - 1), NA_COLS - 1)
    drow = key_rows - r[:, None]
    h_idx = jnp.arange(H)[None, None, None, :, None, None]
    r_idx = (drow + NA_ROWS - 1)[:, None, None, None, :, None]
    c_idx = (dcol + NA_COLS - 1)[None, :, :, None, None, :]
    bias = rel_bias[h_idx, r_idx, c_idx].astype(jnp.float32)
    kg = k.reshape(B, rows, GRID_W, H, dh)
    vg = v.reshape(B, rows, GRID_W, H, dh)
    ridx = key_rows[:, :, None, None]
    cidx = key_cols[None, None, :, :]
    kb = kg[:, ridx, cidx]
    vb = vg[:, ridx, cidx]
    qb = q.reshape(B, rows, ncb, NA_QBLK, H, dh)
    s = jnp.einsum('brnqhd,brknchd->brnqhkc', qb, kb).astype(jnp.float32) / math.sqrt(dh) + bias
    s = jnp.where(col_ok[:, :, None, None, :], s, NEG)
    p = jax.nn.softmax(s.reshape(s.shape[:-2] + (kr * NA_KBLK,)), axis=-1).reshape(s.shape)
    o = jnp.einsum('brnqhkc,brknchd->brnqhd', p.astype(v.dtype), vb)
    return o.reshape(B, L, H, dh)


def band_attention(q, k, v, half):
    Bs, N, H, dh = q.shape
    blk = half
    nb = -(-N // blk)
    n_pad = nb * blk
    qp = jnp.pad(q, ((0, 0), (0, n_pad - N), (0, 0), (0, 0))).reshape(Bs, nb, blk, H, dh)
    pad_kv = ((0, 0), (blk, n_pad - N + blk), (0, 0), (0, 0))
    kp = jnp.pad(k, pad_kv)
    vp = jnp.pad(v, pad_kv)
    idx = jnp.arange(nb)[:, None] * blk + jnp.arange(3 * blk)[None, :]
    kw = kp[:, idx]
    vw = vp[:, idx]
    kpos = idx - blk
    qpos = jnp.arange(nb)[:, None] * blk + jnp.arange(blk)[None, :]
    ok = ((jnp.abs(kpos[:, None, :] - qpos[:, :, None]) <= half)
          & (kpos >= 0)[:, None, :] & (kpos < N)[:, None, :])
    s = jnp.einsum('bnqhd,bnkhd->bnhqk', qp, kw).astype(jnp.float32) / math.sqrt(dh)
    s = jnp.where(ok[:, None], s, NEG)
    m = jnp.max(s, axis=-1, keepdims=True)
    p = jnp.exp(s - m)
    den = jnp.sum(p, axis=-1, keepdims=True)
    o = jnp.einsum('bnhqk,bnkhd->bnqhd', (p / den).astype(v.dtype), vw)
    lse = (m + jnp.log(den))[..., 0].transpose(0, 1, 3, 2)
    return o.reshape(Bs, n_pad, H, dh)[:, :N], lse.reshape(Bs, n_pad, H)[:, :N]


def dilated_attention(q, k, v, window, dil):
    B, L, H, dh = q.shape
    n = L // dil

    def to_sub(t):
        return t.reshape(B, n, dil, H, dh).transpose(0, 2, 1, 3, 4).reshape(B * dil, n, H, dh)

    o, lse = band_attention(to_sub(q), to_sub(k), to_sub(v), (window // dil) // 2)
    o = o.reshape(B, dil, n, H, dh).transpose(0, 2, 1, 3, 4).reshape(B, L, H, dh)
    lse = lse.reshape(B, dil, n, H).transpose(0, 2, 1, 3).reshape(B, L, H)
    return o, lse


def encoder_layer(x, norm_gain, w_in, qn_a, kn_a, rel_bias_a, qn_b, kn_b, w_branch_a, w_branch_b, w_out):
    B, L, _ = x.shape
    h = rms_norm(x, norm_gain)
    proj = jnp.einsum('bld,de->ble', h, w_in)
    parts = jnp.split(proj, _split_points(IN_SIZES), axis=-1)

    def heads(t):
        return t.reshape(B, L, -1, HEAD_DIM)

    q_a = rms_norm(heads(parts[0]), qn_a)
    k_a = rms_norm(heads(parts[1]), kn_a)
    o_a = neighborhood_attention(q_a, k_a, heads(parts[2]), rel_bias_a).reshape(B, L, NA_WIDTH)
    g_a = parts[3]

    outs, lses = [], []
    for gi, (win, dil) in enumerate(DIL_GROUPS):
        base = 4 + 3 * gi
        q = rope(rms_norm(heads(parts[base]), qn_b))
        k = rope(rms_norm(heads(parts[base + 1]), kn_b))
        o, lse = dilated_attention(q, k, heads(parts[base + 2]), win, dil)
        outs.append(o.astype(jnp.float32))
        lses.append(lse)
    wts = jax.nn.softmax(jnp.stack(lses), axis=0)
    o_b = jnp.einsum('gblh,gblhd->blhd', wts, jnp.stack(outs)).astype(x.dtype).reshape(B, L, DIL_WIDTH)
    g_b, m_a, m_b = parts[-3], parts[-2], parts[-1]

    br_a = jnp.einsum('blc,cd->bld', o_a * jax.nn.silu(g_a), w_branch_a)
    br_b = jnp.einsum('blc,cd->bld', o_b * jax.nn.silu(g_b), w_branch_b)
    merged = jax.nn.sigmoid(m_a) * br_a + jax.nn.sigmoid(m_b) * br_b
    return x + jnp.einsum('bld,de->ble', merged, w_out)


def setup_inputs(seed: int = 0) -> dict:
    key = jax.random.key(seed)
    ks = jax.random.split(key, 13)
    f32 = jnp.float32
    nrm = lambda k, shape, scale: jax.random.normal(k, shape, f32) * scale
    return {
        "x_prompt": nrm(ks[0], (BATCH, SEQ, D_MODEL), 1.0),
        "x_sample": nrm(ks[1], (DEC_BATCH, DEC_SEQ, D_MODEL), 1.0),
        "norm_gain": 1.0 + nrm(ks[2], (DEPTH, D_MODEL), 0.05),
        "w_in": nrm(ks[3], (DEPTH, D_MODEL, D_IN), D_MODEL ** -0.5),
        "qn_a": 1.0 + nrm(ks[4], (DEPTH, HEAD_DIM), 0.05),
        "kn_a": 1.0 + nrm(ks[5], (DEPTH, HEAD_DIM), 0.05),
        "rel_bias_a": nrm(ks[6], (DEPTH, NA_HEADS, 2 * NA_ROWS - 1, 2 * NA_COLS - 1), 0.1),
        "qn_b": 1.0 + nrm(ks[7], (DEPTH, HEAD_DIM), 0.05),
        "kn_b": 1.0 + nrm(ks[8], (DEPTH, HEAD_DIM), 0.05),
        "w_branch_a": nrm(ks[9], (DEPTH, NA_WIDTH, D_MODEL), NA_WIDTH ** -0.5),
        "w_branch_b": nrm(ks[10], (DEPTH, DIL_WIDTH, D_MODEL), DIL_WIDTH ** -0.5),
        "w_out": nrm(ks[11], (DEPTH, D_MODEL, D_MODEL), D_MODEL ** -0.5),
    }


def reference(x_prompt, x_sample, norm_gain, w_in, qn_a, kn_a, rel_bias_a, qn_b, kn_b, w_branch_a, w_branch_b, w_out):
    y_prompt = x_prompt
    y_sample = x_sample
    for l in range(DEPTH):
        y_prompt = encoder_layer(y_prompt, norm_gain[l], w_in[l], qn_a[l], kn_a[l], rel_bias_a[l],
                                 qn_b[l], kn_b[l], w_branch_a[l], w_branch_b[l], w_out[l])
        y_sample = encoder_layer(y_sample, norm_gain[l], w_in[l], qn_a[l], kn_a[l], rel_bias_a[l],
                                 qn_b[l], kn_b[l], w_branch_a[l], w_branch_b[l], w_out[l])
    return (y_prompt, y_sample)
```

```python
import functools
import math

import jax
import jax.numpy as jnp
from jax import lax
from jax.experimental import pallas as pl
from jax.experimental.pallas import tpu as pltpu

F32 = jnp.float32
BF16 = jnp.bfloat16

D_MODEL = 1024
HEAD_DIM = 64
GRID_W = 64
NA_HEADS = 8
NA_ROWS = 8
NA_COLS = 16
DIL_GROUPS = ((128, 1), (512, 4), (2048, 16))
ROPE_THETA = 10000.0
EPS = 1e-6
NEG = -1e30

COL = 512
N_COL_TILES = 18
D_IN = COL * N_COL_TILES
LANES = 128
HALO = 64

_NORM_TILES = (0, 1)
_ROPE_TILES = (4, 5, 7, 8, 10, 11)
_SILU_TILES = (3, 13)
_SIGM_TILES = (14, 15, 16, 17)
_Q_TILES = (0, 4, 7, 10)

VMEM_LIMIT = 56 * 1024 * 1024


def _any_of(j, members):
    return functools.reduce(jnp.logical_or, [j == m for m in members])


def _proj_kernel(x_ref, ng_ref, w_ref, gain_ref, cos_ref, sin_ref, bd_ref, o_ref, h_ref):
    j = pl.program_id(1)

    @pl.when(j == 0)
    def _():
        x = x_ref[...]
        ms = jnp.mean(x * x, axis=-1, keepdims=True)
        h_ref[...] = (x * lax.rsqrt(ms + EPS) * ng_ref[...]).astype(BF16)

    acc = jnp.dot(h_ref[...], w_ref[...], preferred_element_type=F32)

    def head_norm(a):
        sq = (a * a).astype(BF16)
        bd = bd_ref[...]
        ms = jnp.concatenate(
            [jnp.dot(sq[:, c:c + 256], bd, preferred_element_type=F32) for c in (0, 256)], axis=1)
        return a * lax.rsqrt(ms + EPS) * gain_ref[0]

    @pl.when(_any_of(j, _NORM_TILES))
    def _():
        o_ref[...] = head_norm(acc).astype(BF16)

    @pl.when(_any_of(j, _ROPE_TILES))
    def _():
        y = head_norm(acc)
        lane = lax.broadcasted_iota(jnp.int32, (1, LANES), 1)
        first_half = (lane % HEAD_DIM) < (HEAD_DIM // 2)
        cos = cos_ref[...]
        sin = sin_ref[...]
        for c in range(0, COL, LANES):
            yc = y[:, c:c + LANES]
            partner = jnp.where(first_half,
                                pltpu.roll(yc, LANES - HEAD_DIM // 2, 1),
                                pltpu.roll(yc, HEAD_DIM // 2, 1))
            o_ref[:, c:c + LANES] = (yc * cos + partner * sin).astype(BF16)

    @pl.when(_any_of(j, _SILU_TILES))
    def _():
        o_ref[...] = (acc * jax.nn.sigmoid(acc)).astype(BF16)

    @pl.when(_any_of(j, _SIGM_TILES))
    def _():
        o_ref[...] = jax.nn.sigmoid(acc).astype(BF16)

    plain = jnp.logical_not(_any_of(j, _NORM_TILES + _ROPE_TILES + _SILU_TILES + _SIGM_TILES))

    @pl.when(plain)
    def _():
        o_ref[...] = acc.astype(BF16)


def _project(x2d, seq_len, ng, w_bf, gains, cos, sin, bd, tm=1024):
    n_tok = x2d.shape[0]
    pos_blocks = seq_len // tm
    return pl.pallas_call(
        _proj_kernel,
        grid=(n_tok // tm, N_COL_TILES),
        in_specs=[
            pl.BlockSpec((tm, D_MODEL), lambda i, j: (i, 0)),
            pl.BlockSpec((1, D_MODEL), lambda i, j: (0, 0)),
            pl.BlockSpec((D_MODEL, COL), lambda i, j: (0, j)),
            pl.BlockSpec((1, 1, COL), lambda i, j: (j, 0, 0)),
            pl.BlockSpec((tm, LANES), lambda i, j: (i % pos_blocks, 0)),
            pl.BlockSpec((tm, LANES), lambda i, j: (i % pos_blocks, 0)),
            pl.BlockSpec((256, 256), lambda i, j: (0, 0)),
        ],
        out_specs=pl.BlockSpec((tm, COL), lambda i, j: (i, j)),
        out_shape=jax.ShapeDtypeStruct((n_tok, D_IN), BF16),
        scratch_shapes=[pltpu.VMEM((tm, D_MODEL), BF16)],
        compiler_params=pltpu.CompilerParams(
            dimension_semantics=("parallel", "arbitrary"), vmem_limit_bytes=VMEM_LIMIT),
        name="in_proj",
    )(x2d, ng, w_bf, gains, cos, sin, bd)


def _na_kernel(q_ref, k_ref, v_ref, g_ref, bias_ref, o_ref, *, rows, rq):
    rb = pl.program_id(2)
    lane = lax.broadcasted_iota(jnp.int32, (1, LANES), 1)
    lo = lane < HEAD_DIM
    kwin = NA_ROWS * GRID_W
    for rr in range(rq):
        r = rb * rq + rr
        rs = jnp.clip(r - NA_ROWS // 2, 0, rows - NA_ROWS)
        shift = rs - r + (NA_ROWS - 1)
        start = pl.multiple_of(rs * GRID_W, GRID_W)
        kw = k_ref[pl.ds(start, kwin), :]
        vw = v_ref[pl.ds(start, kwin), :]
        q = q_ref[rr * GRID_W:(rr + 1) * GRID_W, :]
        outs = []
        for h in range(2):
            qh = jnp.where(lo if h == 0 else jnp.logical_not(lo), q, jnp.zeros_like(q))
            s = lax.dot_general(qh, kw, (((1,), (1,)), ((), ())), preferred_element_type=F32)
            s = s + bias_ref[h, shift]
            m = jnp.max(s, axis=-1, keepdims=True)
            p = jnp.exp(s - m)
            l = jnp.sum(p, axis=-1, keepdims=True)
            outs.append(jnp.dot(p.astype(BF16), vw, preferred_element_type=F32) / l)
        o = jnp.where(lo, outs[0], outs[1])
        gate = g_ref[rr * GRID_W:(rr + 1) * GRID_W, :].astype(F32)
        o_ref[rr * GRID_W:(rr + 1) * GRID_W, :] = (o * gate).astype(BF16)


def _neighbourhood(proj3, bias_tab, rq=8):
    batch, seq_len, _ = proj3.shape
    rows = seq_len // GRID_W
    tq = rq * GRID_W
    per = COL // LANES
    return pl.pallas_call(
        functools.partial(_na_kernel, rows=rows, rq=rq),
        grid=(batch, per, rows // rq),
        in_specs=[
            pl.BlockSpec((None, tq, LANES), lambda b, hp, rb: (b, rb, 0 * per + hp)),
            pl.BlockSpec((None, seq_len, LANES), lambda b, hp, rb: (b, 0, 1 * per + hp)),
            pl.BlockSpec((None, seq_len, LANES), lambda b, hp, rb: (b, 0, 2 * per + hp)),
            pl.BlockSpec((None, tq, LANES), lambda b, hp, rb: (b, rb, 3 * per + hp)),
            pl.BlockSpec((2, NA_ROWS, GRID_W, NA_ROWS * GRID_W), lambda b, hp, rb: (hp, 0, 0, 0)),
        ],
        out_specs=pl.BlockSpec((None, tq, LANES), lambda b, hp, rb: (b, rb, hp)),
        out_shape=jax.ShapeDtypeStruct((batch, seq_len, COL), BF16),
        compiler_params=pltpu.CompilerParams(
            dimension_semantics=("parallel", "parallel", "arbitrary"), vmem_limit_bytes=VMEM_LIMIT),
        name="na_attn",
    )(proj3, proj3, proj3, proj3, bias_tab)


def _na_bias_table(rel_bias):
    c = jnp.arange(GRID_W)
    cs = jnp.clip(c - NA_COLS // 2, 0, GRID_W - NA_COLS)
    ok = (c[None, :] >= cs[:, None]) & (c[None, :] < cs[:, None] + NA_COLS)
    cidx = jnp.clip(c[None, :] - c[:, None], -(NA_COLS - 1), NA_COLS - 1) + NA_COLS - 1
    ridx = jnp.arange(NA_ROWS)[None, :] + jnp.arange(NA_ROWS)[:, None]
    tab = rel_bias[:, ridx[:, :, None, None], cidx[None, None, :, :]].astype(F32)
    tab = jnp.where(ok[None, None, None], tab, NEG)
    return tab.transpose(0, 1, 3, 2, 4).reshape(NA_HEADS, NA_ROWS, GRID_W, NA_ROWS * GRID_W)


def _band_kernel(*refs, n, tq, merge):
    q_ref, kp_ref, kc_ref, kn_ref, vp_ref, vc_ref, vn_ref = refs[:7]
    if merge:
        o1_ref, l1_ref, o2_ref, l2_ref, g_ref, out_ref = refs[7:]
    else:
        out_ref, lse_ref = refs[7:]
    i = pl.program_id(2)
    tk = tq + 2 * HALO
    qpos = i * tq + lax.broadcasted_iota(jnp.int32, (tq, tk), 0)
    kpos = i * tq - HALO + lax.broadcasted_iota(jnp.int32, (tq, tk), 1)
    ok = (jnp.abs(kpos - qpos) <= HALO) & (kpos >= 0) & (kpos < n)
    mask_bias = jnp.where(ok, 0.0, NEG).astype(F32)
    lane = lax.broadcasted_iota(jnp.int32, (1, LANES), 1)
    lo = lane < HEAD_DIM
    for hp in range(COL // LANES):
        sl = slice(hp * LANES, (hp + 1) * LANES)
        q = q_ref[:, sl]
        k = jnp.concatenate([kp_ref[:, sl], kc_ref[:, sl], kn_ref[:, sl]], axis=0)
        v = jnp.concatenate([vp_ref[:, sl], vc_ref[:, sl], vn_ref[:, sl]], axis=0)
        outs, lses = [], []
        for h in range(2):
            qh = jnp.where(lo if h == 0 else jnp.logical_not(lo), q, jnp.zeros_like(q))
            s = lax.dot_general(qh, k, (((1,), (1,)), ((), ())), preferred_element_type=F32)
            s = s + mask_bias
            m = jnp.max(s, axis=-1, keepdims=True)
            p = jnp.exp(s - m)
            l = jnp.sum(p, axis=-1, keepdims=True)
            outs.append(jnp.dot(p.astype(BF16), v, preferred_element_type=F32) / l)
            lses.append(m + jnp.log(l))
        o = jnp.where(lo, outs[0], outs[1])
        lse = jnp.where(lo, lses[0], lses[1])
        if not merge:
            out_ref[:, sl] = o.astype(BF16)
            lse_ref[:, sl] = lse
        else:
            l1 = l1_ref[:, sl]
            l2 = l2_ref[:, sl]
            top = jnp.maximum(lse, jnp.maximum(l1, l2))
            w0 = jnp.exp(lse - top)
            w1 = jnp.exp(l1 - top)
            w2 = jnp.exp(l2 - top)
            num = w0 * o + w1 * o1_ref[:, sl].astype(F32) + w2 * o2_ref[:, sl].astype(F32)
            ob = num / (w0 + w1 + w2)
            out_ref[:, sl] = (ob * g_ref[:, sl].astype(F32)).astype(BF16)


def _band_group(proj3, group, dil, extra=None, tq=128):
    batch, seq_len, _ = proj3.shape
    n = seq_len // dil
    view = proj3.reshape(batch, n, dil * D_IN)
    cq, ck, cv = 4 + 3 * group, 5 + 3 * group, 6 + 3 * group
    hb = tq // HALO
    last = n // HALO - 1

    def cur(col):
        return pl.BlockSpec((None, tq, COL), lambda b, j, i: (b, i, j * N_COL_TILES + col))

    def prev(col):
        return pl.BlockSpec((None, HALO, COL),
                            lambda b, j, i: (b, jnp.maximum(i * hb - 1, 0), j * N_COL_TILES + col))

    def nxt(col):
        return pl.BlockSpec((None, HALO, COL),
                            lambda b, j, i: (b, jnp.minimum((i + 1) * hb, last), j * N_COL_TILES + col))

    tok_spec = pl.BlockSpec((None, tq, COL), lambda b, j, i: (b, i, j))
    in_specs = [cur(cq), prev(ck), cur(ck), nxt(ck), prev(cv), cur(cv), nxt(cv)]
    args = [view] * 7
    merge = extra is not None
    if merge:
        assert dil == 1
        in_specs += [tok_spec] * 4 + [cur(13)]
        args += list(extra) + [view]
        out_specs = tok_spec
        out_shape = jax.ShapeDtypeStruct((batch, n, dil * COL), BF16)
    else:
        out_specs = [tok_spec, tok_spec]
        out_shape = [jax.ShapeDtypeStruct((batch, n, dil * COL), BF16),
                     jax.ShapeDtypeStruct((batch, n, dil * COL), F32)]
    res = pl.pallas_call(
        functools.partial(_band_kernel, n=n, tq=tq, merge=merge),
        grid=(batch, dil, n // tq),
        in_specs=in_specs,
        out_specs=out_specs,
        out_shape=out_shape,
        compiler_params=pltpu.CompilerParams(
            dimension_semantics=("parallel", "parallel", "arbitrary"), vmem_limit_bytes=VMEM_LIMIT),
        name="band_d%d" % dil,
    )(*args)
    if merge:
        return res.reshape(batch, seq_len, COL)
    return [r.reshape(batch, seq_len, COL) for r in res]


def _out_kernel(x_ref, ua_ref, ub_ref, sa_ref, sb_ref, wa_ref, wb_ref, wo_ref, y_ref):
    br_a = jnp.dot(ua_ref[...], wa_ref[...], preferred_element_type=F32)
    br_b = jnp.dot(ub_ref[...], wb_ref[...], preferred_element_type=F32)
    merged = sa_ref[...].astype(F32) * br_a + sb_ref[...].astype(F32) * br_b
    y_ref[...] = x_ref[...] + jnp.dot(merged.astype(BF16), wo_ref[...], preferred_element_type=F32)


def _output(x2d, ua, ub, proj, wa, wb, wo, tm=512):
    n_tok = x2d.shape[0]
    sig_a = (14 * COL) // D_MODEL
    sig_b = (16 * COL) // D_MODEL
    return pl.pallas_call(
        _out_kernel,
        grid=(n_tok // tm,),
        in_specs=[
            pl.BlockSpec((tm, D_MODEL), lambda i: (i, 0)),
            pl.BlockSpec((tm, COL), lambda i: (i, 0)),
            pl.BlockSpec((tm, COL), lambda i: (i, 0)),
            pl.BlockSpec((tm, D_MODEL), lambda i: (i, sig_a)),
            pl.BlockSpec((tm, D_MODEL), lambda i: (i, sig_b)),
            pl.BlockSpec((COL, D_MODEL), lambda i: (0, 0)),
            pl.BlockSpec((COL, D_MODEL), lambda i: (0, 0)),
            pl.BlockSpec((D_MODEL, D_MODEL), lambda i: (0, 0)),
        ],
        out_specs=pl.BlockSpec((tm, D_MODEL), lambda i: (i, 0)),
        out_shape=jax.ShapeDtypeStruct((n_tok, D_MODEL), F32),
        compiler_params=pltpu.CompilerParams(
            dimension_semantics=("parallel",), vmem_limit_bytes=VMEM_LIMIT),
        name="out_proj",
    )(x2d, ua, ub, proj, proj, wa, wb, wo)


def _rope_tables(seq_len):
    inv = ROPE_THETA ** (-jnp.arange(0, HEAD_DIM, 2, dtype=F32) / HEAD_DIM)
    ang = jnp.arange(seq_len, dtype=F32)[:, None] * inv[None, :]
    cos = jnp.cos(ang)
    sin = jnp.sin(ang)
    reps = LANES // HEAD_DIM
    cos_t = jnp.tile(jnp.concatenate([cos, cos], axis=1), (1, reps))
    sin_t = jnp.tile(jnp.concatenate([-sin, sin], axis=1), (1, reps))
    return cos_t, sin_t


def _tile_gains(qn_a, kn_a, qn_b, kn_b):
    scale = 1.0 / math.sqrt(HEAD_DIM)
    heads = COL // HEAD_DIM
    rows = []
    for t in range(N_COL_TILES):
        if t == 0:
            g = qn_a
        elif t == 1:
            g = kn_a
        elif t in (4, 7, 10):
            g = qn_b
        elif t in (5, 8, 11):
            g = kn_b
        else:
            g = jnp.ones((HEAD_DIM,), F32)
        g = g.astype(F32)
        if t in _Q_TILES:
            g = g * scale
        rows.append(jnp.tile(g, heads))
    return jnp.stack(rows)[:, None, :]


def _layer(x, ng, w_bf, gains, bias_tab, bd, wa, wb, wo):
    batch, seq_len, _ = x.shape
    x2d = x.reshape(batch * seq_len, D_MODEL)
    cos, sin = _rope_tables(seq_len)
    proj = _project(x2d, seq_len, ng, w_bf, gains, cos, sin, bd)
    proj3 = proj.reshape(batch, seq_len, D_IN)
    ua = _neighbourhood(proj3, bias_tab)
    o2, l2 = _band_group(proj3, 2, DIL_GROUPS[2][1])
    o1, l1 = _band_group(proj3, 1, DIL_GROUPS[1][1])
    ub = _band_group(proj3, 0, DIL_GROUPS[0][1], extra=(o1, l1, o2, l2))
    y = _output(x2d, ua.reshape(-1, COL), ub.reshape(-1, COL), proj, wa, wb, wo)
    return y.reshape(batch, seq_len, D_MODEL)


def kernel(x_prompt, x_sample, norm_gain, w_in, qn_a, kn_a, rel_bias_a, qn_b, kn_b,
           w_branch_a, w_branch_b, w_out):
    depth = norm_gain.shape[0]
    blk = jnp.arange(256) // HEAD_DIM
    bd = jnp.where(blk[:, None] == blk[None, :], 1.0 / HEAD_DIM, 0.0).astype(BF16)
    y_prompt, y_sample = x_prompt, x_sample
    for l in range(depth):
        ng = norm_gain[l].astype(F32)[None, :]
        w_bf = w_in[l].astype(BF16)
        gains = _tile_gains(qn_a[l], kn_a[l], qn_b[l], kn_b[l])
        bias_tab = _na_bias_table(rel_bias_a[l])
        wa = w_branch_a[l].astype(BF16)
        wb = w_branch_b[l].astype(BF16)
        wo = w_out[l].astype(BF16)
        y_prompt = _layer(y_prompt, ng, w_bf, gains, bias_tab, bd, wa, wb, wo)
        y_sample = _layer(y_sample, ng, w_bf, gains, bias_tab, bd, wa, wb, wo)
    return (y_prompt, y_sample)
```

```python
import functools
import math

import jax
import jax.numpy as jnp
from jax import lax
from jax.experimental import pallas as pl
from jax.experimental.pallas import tpu as pltpu

F32 = jnp.float32
BF16 = jnp.bfloat16

D_MODEL = 1024
HEAD_DIM = 64
GRID_W = 64
NA_HEADS = 8
NA_ROWS = 8
NA_COLS = 16
DIL_GROUPS = ((128, 1), (512, 4), (2048, 16))
ROPE_THETA = 10000.0
EPS = 1e-6
NEG = -1e30

COL = 512
N_COL_TILES = 18
D_IN = COL * N_COL_TILES
LANES = 128
PAIRS = COL // LANES
HALO = 64
TILE = 2048
BLOCKS = TILE // HALO
TQ = 2 * HALO
ROW_CHUNK = 256

_NORM_TILES = (0, 1)
_ROPE_TILES = (4, 5, 7, 8, 10, 11)
_PLAIN_TILES = (2, 6)
_PLAIN_PERM_TILES = (9, 12)
_SILU_TILES = (3, 13)
_SIGM_TILES = (14, 15, 16, 17)
_Q_TILES = (0, 4, 7, 10)
_PERM_TILES = {1: (4, 5), 4: (7, 8, 9), 16: (10, 11, 12)}

VMEM_LIMIT = 56 * 1024 * 1024


def _any_of(j, members):
    return functools.reduce(jnp.logical_or, [j == m for m in members])


def _proj_kernel(x_ref, ng_ref, w_ref, gain_ref, cos_ref, sin_ref, bd_ref, o_ref, h_ref, perm_ref):
    j = pl.program_id(1)
    chunks = range(0, TILE, ROW_CHUNK)

    @pl.when(j == 0)
    def _():
        for r0 in chunks:
            x = x_ref[r0:r0 + ROW_CHUNK, :]
            ms = jnp.mean(x * x, axis=-1, keepdims=True)
            h_ref[r0:r0 + ROW_CHUNK, :] = (x * lax.rsqrt(ms + EPS) * ng_ref[...]).astype(BF16)

    def head_norm(a):
        sq = (a * a).astype(BF16)
        bd = bd_ref[...]
        ms = jnp.concatenate(
            [jnp.dot(sq[:, c:c + 256], bd, preferred_element_type=F32) for c in (0, 256)], axis=1)
        return a * lax.rsqrt(ms + EPS) * gain_ref[0]

    def rope(a, r0):
        y = head_norm(a)
        lane = lax.broadcasted_iota(jnp.int32, (1, LANES), 1)
        first_half = (lane % HEAD_DIM) < (HEAD_DIM // 2)
        cos = cos_ref[r0:r0 + ROW_CHUNK, :]
        sin = sin_ref[r0:r0 + ROW_CHUNK, :]
        parts = []
        for c in range(0, COL, LANES):
            yc = y[:, c:c + LANES]
            partner = jnp.where(first_half,
                                pltpu.roll(yc, LANES - HEAD_DIM // 2, 1),
                                pltpu.roll(yc, HEAD_DIM // 2, 1))
            parts.append(yc * cos + partner * sin)
        return jnp.concatenate(parts, axis=1)

    def run(members, epilogue, to_perm):
        @pl.when(_any_of(j, members))
        def _():
            for r0 in chunks:
                a = jnp.dot(h_ref[r0:r0 + ROW_CHUNK, :], w_ref[...], preferred_element_type=F32)
                y = epilogue(a, r0)
                if to_perm:
                    for cp in range(PAIRS):
                        perm_ref[cp, r0:r0 + ROW_CHUNK, :] = y[:, cp * LANES:(cp + 1) * LANES]
                else:
                    o_ref[r0:r0 + ROW_CHUNK, :] = y.astype(BF16)

    run(_NORM_TILES, lambda a, r0: head_norm(a), False)
    run(_ROPE_TILES, rope, True)
    run(_PLAIN_TILES, lambda a, r0: a, False)
    run(_PLAIN_PERM_TILES, lambda a, r0: a, True)
    run(_SILU_TILES, lambda a, r0: a * jax.nn.sigmoid(a), False)
    run(_SIGM_TILES, lambda a, r0: jax.nn.sigmoid(a), False)

    for dil, members in _PERM_TILES.items():
        @pl.when(_any_of(j, members))
        def _(dil=dil):
            for cp in range(PAIRS):
                sl = slice(cp * LANES, (cp + 1) * LANES)
                if dil == 1:
                    for r0 in chunks:
                        o_ref[r0:r0 + ROW_CHUNK, sl] = perm_ref[cp, r0:r0 + ROW_CHUNK, :].astype(BF16)
                else:
                    for c in range(BLOCKS // dil):
                        for ph in range(dil):
                            blk = c * dil + ph
                            rows = perm_ref[cp, pl.ds(c * HALO * dil + ph, HALO, stride=dil), :]
                            o_ref[blk * HALO:(blk + 1) * HALO, sl] = rows.astype(BF16)


def _project(x2d, seq_len, ng, w_bf, gains, cos, sin, bd):
    n_tok = x2d.shape[0]
    pos_blocks = seq_len // TILE
    return pl.pallas_call(
        _proj_kernel,
        grid=(n_tok // TILE, N_COL_TILES),
        in_specs=[
            pl.BlockSpec((TILE, D_MODEL), lambda i, j: (i, 0)),
            pl.BlockSpec((1, D_MODEL), lambda i, j: (0, 0)),
            pl.BlockSpec((D_MODEL, COL), lambda i, j: (0, j)),
            pl.BlockSpec((1, 1, COL), lambda i, j: (j, 0, 0)),
            pl.BlockSpec((TILE, LANES), lambda i, j: (i % pos_blocks, 0)),
            pl.BlockSpec((TILE, LANES), lambda i, j: (i % pos_blocks, 0)),
            pl.BlockSpec((256, 256), lambda i, j: (0, 0)),
        ],
        out_specs=pl.BlockSpec((TILE, COL), lambda i, j: (i, j)),
        out_shape=jax.ShapeDtypeStruct((n_tok, D_IN), BF16),
        scratch_shapes=[pltpu.VMEM((TILE, D_MODEL), BF16), pltpu.VMEM((PAIRS, TILE, LANES), F32)],
        compiler_params=pltpu.CompilerParams(
            dimension_semantics=("parallel", "arbitrary"), vmem_limit_bytes=VMEM_LIMIT),
        name="in_proj",
    )(x2d, ng, w_bf, gains, cos, sin, bd)


def _attend_pair(q, k, v, bias, lo):
    nq = q.shape[0]
    zero = jnp.zeros_like(q)
    qq = jnp.concatenate([jnp.where(lo, q, zero), jnp.where(lo, zero, q)], axis=0)
    s = lax.dot_general(qq, k, (((1,), (1,)), ((), ())), preferred_element_type=F32) + bias
    m = jnp.max(s, axis=-1, keepdims=True)
    p = jnp.exp(s - m)
    l = jnp.sum(p, axis=-1, keepdims=True)
    o2 = jnp.dot(p.astype(BF16), v, preferred_element_type=F32) / l
    lse2 = m + jnp.log(l)
    return jnp.where(lo, o2[:nq], o2[nq:]), jnp.where(lo, lse2[:nq], lse2[nq:])


def _lane_lo():
    return lax.broadcasted_iota(jnp.int32, (1, LANES), 1) < HEAD_DIM


def _na_kernel(q_ref, k_ref, v_ref, g_ref, bias_ref, o_ref, *, rows, rq):
    rb = pl.program_id(2)
    lo = _lane_lo()
    kwin = NA_ROWS * GRID_W
    for rr in range(rq):
        r = rb * rq + rr
        rs = jnp.clip(r - NA_ROWS // 2, 0, rows - NA_ROWS)
        shift = rs - r + (NA_ROWS - 1)
        start = pl.multiple_of(rs * GRID_W, GRID_W)
        rows_q = slice(rr * GRID_W, (rr + 1) * GRID_W)
        o, _ = _attend_pair(q_ref[rows_q, :], k_ref[pl.ds(start, kwin), :], v_ref[pl.ds(start, kwin), :],
                            bias_ref[shift], lo)
        o_ref[rows_q, :] = (o * g_ref[rows_q, :].astype(F32)).astype(BF16)


def _neighbourhood(proj3, bias_tab, rq=8):
    batch, seq_len, _ = proj3.shape
    rows = seq_len // GRID_W
    tq = rq * GRID_W
    return pl.pallas_call(
        functools.partial(_na_kernel, rows=rows, rq=rq),
        grid=(batch, PAIRS, rows // rq),
        in_specs=[
            pl.BlockSpec((None, tq, LANES), lambda b, hp, rb: (b, rb, 0 * PAIRS + hp)),
            pl.BlockSpec((None, seq_len, LANES), lambda b, hp, rb: (b, 0, 1 * PAIRS + hp)),
            pl.BlockSpec((None, seq_len, LANES), lambda b, hp, rb: (b, 0, 2 * PAIRS + hp)),
            pl.BlockSpec((None, tq, LANES), lambda b, hp, rb: (b, rb, 3 * PAIRS + hp)),
            pl.BlockSpec((None, NA_ROWS, 2 * GRID_W, NA_ROWS * GRID_W), lambda b, hp, rb: (hp, 0, 0, 0)),
        ],
        out_specs=pl.BlockSpec((None, tq, LANES), lambda b, hp, rb: (b, rb, hp)),
        out_shape=jax.ShapeDtypeStruct((batch, seq_len, COL), BF16),
        compiler_params=pltpu.CompilerParams(
            dimension_semantics=("parallel", "parallel", "arbitrary"), vmem_limit_bytes=VMEM_LIMIT),
        name="na_attn",
    )(proj3, proj3, proj3, proj3, bias_tab)


def _na_bias_table(rel_bias):
    heads, n_dr, n_dc = rel_bias.shape
    pad = GRID_W - NA_COLS
    padded = jnp.pad(rel_bias.astype(F32), ((0, 0), (0, 0), (pad, pad)))
    toeplitz = jnp.stack([padded[:, :, GRID_W - 1 - qc:2 * GRID_W - 1 - qc] for qc in range(GRID_W)], axis=2)
    c = jnp.arange(GRID_W)
    cs = jnp.clip(c - NA_COLS // 2, 0, GRID_W - NA_COLS)
    ok = (c[None, :] >= cs[:, None]) & (c[None, :] < cs[:, None] + NA_COLS)
    toeplitz = jnp.where(ok[None, None], toeplitz, NEG)
    tab = jnp.stack([toeplitz[:, sh:sh + NA_ROWS] for sh in range(NA_ROWS)], axis=1)
    tab = tab.transpose(0, 1, 3, 2, 4).reshape(heads // 2, 2, NA_ROWS, GRID_W, NA_ROWS * GRID_W)
    return tab.transpose(0, 2, 1, 3, 4).reshape(heads // 2, NA_ROWS, 2 * GRID_W, NA_ROWS * GRID_W)


def _band_bias(first, last):
    nk = TQ + 2 * HALO
    qpos = lax.broadcasted_iota(jnp.int32, (2 * TQ, nk), 0) % TQ
    col = lax.broadcasted_iota(jnp.int32, (2 * TQ, nk), 1)
    bias = jnp.where(jnp.abs(col - HALO - qpos) <= HALO, 0.0, NEG).astype(F32)
    if first is not None:
        bias = bias + jnp.where(col < HALO, jnp.where(first, NEG, 0.0).astype(F32), 0.0)
    if last is not None:
        bias = bias + jnp.where(col >= TQ + HALO, jnp.where(last, NEG, 0.0).astype(F32), 0.0)
    return bias


def _band_dilated_kernel(q_ref, kp_ref, kc_ref, kn_ref, vp_ref, vc_ref, vn_ref, out_ref, lse_ref, o_scr, l_scr,
                         *, dil, tiles):
    i = pl.program_id(1)
    lo = _lane_lo()
    units = BLOCKS // dil // 2
    first, last = i == 0, i == tiles - 1
    biases = [_band_bias(first if a == 0 else None, last if a == units - 1 else None) for a in range(units)]

    def rows(chunk, ph):
        return pl.ds(pl.multiple_of((chunk * dil + ph) * HALO, HALO), HALO)

    def phase(ph, carry):
        halo = pl.ds(pl.multiple_of(ph * HALO, HALO), HALO)
        for a in range(units):
            for hp in range(PAIRS):
                sl = slice(hp * LANES, (hp + 1) * LANES)

                def window(prev_ref, cur_ref, next_ref):
                    head = prev_ref[halo, sl] if a == 0 else cur_ref[rows(2 * a - 1, ph), sl]
                    tail = next_ref[halo, sl] if a == units - 1 else cur_ref[rows(2 * a + 2, ph), sl]
                    return jnp.concatenate(
                        [head, cur_ref[rows(2 * a, ph), sl], cur_ref[rows(2 * a + 1, ph), sl], tail], axis=0)

                q = jnp.concatenate([q_ref[rows(2 * a, ph), sl], q_ref[rows(2 * a + 1, ph), sl]], axis=0)
                o, lse = _attend_pair(q, window(kp_ref, kc_ref, kn_ref), window(vp_ref, vc_ref, vn_ref),
                                      biases[a], lo)
                tok = pl.ds(a * TQ * dil + ph, TQ, stride=dil)
                o_scr[hp, tok, :] = o
                l_scr[hp, tok, :] = lse
        return carry

    lax.fori_loop(0, dil, phase, 0)
    for hp in range(PAIRS):
        out_ref[:, hp * LANES:(hp + 1) * LANES] = o_scr[hp].astype(BF16)
        lse_ref[:, hp * LANES:(hp + 1) * LANES] = l_scr[hp]


def _band_dilated(proj3, group, dil):
    batch, seq_len, _ = proj3.shape
    tiles = seq_len // TILE
    cq, ck, cv = 4 + 3 * group, 5 + 3 * group, 6 + 3 * group
    halo_rows = dil * HALO
    per_tile = TILE // halo_rows
    last_unit = seq_len // halo_rows - 1

    def cur(col):
        return pl.BlockSpec((None, TILE, COL), lambda b, i: (b, i, col))

    def prev(col):
        return pl.BlockSpec((None, halo_rows, COL), lambda b, i: (b, jnp.maximum(i * per_tile - 1, 0), col))

    def nxt(col):
        return pl.BlockSpec((None, halo_rows, COL),
                            lambda b, i: (b, jnp.minimum((i + 1) * per_tile, last_unit), col))

    tok_spec = pl.BlockSpec((None, TILE, COL), lambda b, i: (b, i, 0))
    return pl.pallas_call(
        functools.partial(_band_dilated_kernel, dil=dil, tiles=tiles),
        grid=(batch, tiles),
        in_specs=[cur(cq), prev(ck), cur(ck), nxt(ck), prev(cv), cur(cv), nxt(cv)],
        out_specs=[tok_spec, tok_spec],
        out_shape=[jax.ShapeDtypeStruct((batch, seq_len, COL), BF16),
                   jax.ShapeDtypeStruct((batch, seq_len, COL), F32)],
        scratch_shapes=[pltpu.VMEM((PAIRS, TILE, LANES), F32), pltpu.VMEM((PAIRS, TILE, LANES), F32)],
        compiler_params=pltpu.CompilerParams(
            dimension_semantics=("parallel", "arbitrary"), vmem_limit_bytes=VMEM_LIMIT),
        name="band_d%d" % dil,
    )(*([proj3] * 7))


def _band_merge_kernel(q_ref, kp_ref, kc_ref, kn_ref, vp_ref, vc_ref, vn_ref,
                       o1_ref, l1_ref, o2_ref, l2_ref, g_ref, out_ref, *, tm, tiles):
    i = pl.program_id(1)
    lo = _lane_lo()
    units = tm // TQ
    first, last = i == 0, i == tiles - 1

    def unit(a, bias, kv_window):
        r0 = a * TQ if isinstance(a, int) else pl.multiple_of(a * TQ, TQ)
        qrows = pl.ds(r0, TQ)
        for hp in range(PAIRS):
            sl = slice(hp * LANES, (hp + 1) * LANES)
            k = kv_window(kp_ref, kc_ref, kn_ref, r0, sl)
            v = kv_window(vp_ref, vc_ref, vn_ref, r0, sl)
            o, lse = _attend_pair(q_ref[qrows, sl], k, v, bias, lo)
            l1 = l1_ref[qrows, sl]
            l2 = l2_ref[qrows, sl]
            top = jnp.maximum(lse, jnp.maximum(l1, l2))
            w0 = jnp.exp(lse - top)
            w1 = jnp.exp(l1 - top)
            w2 = jnp.exp(l2 - top)
            num = w0 * o + w1 * o1_ref[qrows, sl].astype(F32) + w2 * o2_ref[qrows, sl].astype(F32)
            ob = num / (w0 + w1 + w2)
            out_ref[qrows, sl] = (ob * g_ref[qrows, sl].astype(F32)).astype(BF16)

    def win_first(p_ref, c_ref, n_ref, r0, sl):
        return jnp.concatenate([p_ref[:, sl], c_ref[0:TQ + HALO, sl]], axis=0)

    def win_mid(p_ref, c_ref, n_ref, r0, sl):
        return c_ref[pl.ds(pl.multiple_of(r0 - HALO, HALO), TQ + 2 * HALO), sl]

    def win_last(p_ref, c_ref, n_ref, r0, sl):
        return jnp.concatenate([c_ref[tm - TQ - HALO:tm, sl], n_ref[:, sl]], axis=0)

    unit(0, _band_bias(first, None), win_first)
    mid_bias = _band_bias(None, None)

    def body(a, carry):
        unit(a, mid_bias, win_mid)
        return carry

    lax.fori_loop(1, units - 1, body, 0)
    unit(units - 1, _band_bias(None, last), win_last)


def _band_merge(proj3, o1, l1, o2, l2, tm=1024):
    batch, seq_len, _ = proj3.shape
    tiles = seq_len // tm
    hb = tm // HALO
    last_halo = seq_len // HALO - 1
    cq, ck, cv, cg = 4, 5, 6, 13

    def cur(col):
        return pl.BlockSpec((None, tm, COL), lambda b, i: (b, i, col))

    def prev(col):
        return pl.BlockSpec((None, HALO, COL), lambda b, i: (b, jnp.maximum(i * hb - 1, 0), col))

    def nxt(col):
        return pl.BlockSpec((None, HALO, COL), lambda b, i: (b, jnp.minimum((i + 1) * hb, last_halo), col))

    tok_spec = pl.BlockSpec((None, tm, COL), lambda b, i: (b, i, 0))
    return pl.pallas_call(
        functools.partial(_band_merge_kernel, tm=tm, tiles=tiles),
        grid=(batch, tiles),
        in_specs=[cur(cq), prev(ck), cur(ck), nxt(ck), prev(cv), cur(cv), nxt(cv)] + [tok_spec] * 4 + [cur(cg)],
        out_specs=tok_spec,
        out_shape=jax.ShapeDtypeStruct((batch, seq_len, COL), BF16),
        compiler_params=pltpu.CompilerParams(
            dimension_semantics=("parallel", "arbitrary"), vmem_limit_bytes=VMEM_LIMIT),
        name="band_merge",
    )(*([proj3] * 7), o1, l1, o2, l2, proj3)


def _out_kernel(x_ref, ua_ref, ub_ref, sa_ref, sb_ref, wa_ref, wb_ref, wo_ref, y_ref):
    br_a = jnp.dot(ua_ref[...], wa_ref[...], preferred_element_type=F32)
    br_b = jnp.dot(ub_ref[...], wb_ref[...], preferred_element_type=F32)
    merged = sa_ref[...].astype(F32) * br_a + sb_ref[...].astype(F32) * br_b
    y_ref[...] = x_ref[...] + jnp.dot(merged.astype(BF16), wo_ref[...], preferred_element_type=F32)


def _output(x2d, ua, ub, proj, wa, wb, wo, tm=512):
    n_tok = x2d.shape[0]
    sig_a = (14 * COL) // D_MODEL
    sig_b = (16 * COL) // D_MODEL
    return pl.pallas_call(
        _out_kernel,
        grid=(n_tok // tm,),
        in_specs=[
            pl.BlockSpec((tm, D_MODEL), lambda i: (i, 0)),
            pl.BlockSpec((tm, COL), lambda i: (i, 0)),
            pl.BlockSpec((tm, COL), lambda i: (i, 0)),
            pl.BlockSpec((tm, D_MODEL), lambda i: (i, sig_a)),
            pl.BlockSpec((tm, D_MODEL), lambda i: (i, sig_b)),
            pl.BlockSpec((COL, D_MODEL), lambda i: (0, 0)),
            pl.BlockSpec((COL, D_MODEL), lambda i: (0, 0)),
            pl.BlockSpec((D_MODEL, D_MODEL), lambda i: (0, 0)),
        ],
        out_specs=pl.BlockSpec((tm, D_MODEL), lambda i: (i, 0)),
        out_shape=jax.ShapeDtypeStruct((n_tok, D_MODEL), F32),
        compiler_params=pltpu.CompilerParams(
            dimension_semantics=("parallel",), vmem_limit_bytes=VMEM_LIMIT),
        name="out_proj",
    )(x2d, ua, ub, proj, proj, wa, wb, wo)


def _rope_tables(seq_len):
    inv = ROPE_THETA ** (-jnp.arange(0, HEAD_DIM, 2, dtype=F32) / HEAD_DIM)
    ang = jnp.arange(seq_len, dtype=F32)[:, None] * inv[None, :]
    cos = jnp.cos(ang)
    sin = jnp.sin(ang)
    reps = LANES // HEAD_DIM
    cos_t = jnp.tile(jnp.concatenate([cos, cos], axis=1), (1, reps))
    sin_t = jnp.tile(jnp.concatenate([-sin, sin], axis=1), (1, reps))
    return cos_t, sin_t


def _tile_gains(qn_a, kn_a, qn_b, kn_b):
    scale = 1.0 / math.sqrt(HEAD_DIM)
    heads = COL // HEAD_DIM
    rows = []
    for t in range(N_COL_TILES):
        if t == 0:
            g = qn_a
        elif t == 1:
            g = kn_a
        elif t in (4, 7, 10):
            g = qn_b
        elif t in (5, 8, 11):
            g = kn_b
        else:
            g = jnp.ones((HEAD_DIM,), F32)
        g = g.astype(F32)
        if t in _Q_TILES:
            g = g * scale
        rows.append(jnp.tile(g, heads))
    return jnp.stack(rows)[:, None, :]


def _layer(x, ng, w_bf, gains, bias_tab, bd, wa, wb, wo):
    batch, seq_len, _ = x.shape
    assert seq_len % TILE == 0
    x2d = x.reshape(batch * seq_len, D_MODEL)
    cos, sin = _rope_tables(seq_len)
    proj = _project(x2d, seq_len, ng, w_bf, gains, cos, sin, bd)
    proj3 = proj.reshape(batch, seq_len, D_IN)
    ua = _neighbourhood(proj3, bias_tab)
    o2, l2 = _band_dilated(proj3, 2, DIL_GROUPS[2][1])
    o1, l1 = _band_dilated(proj3, 1, DIL_GROUPS[1][1])
    ub = _band_merge(proj3, o1, l1, o2, l2)
    y = _output(x2d, ua.reshape(-1, COL), ub.reshape(-1, COL), proj, wa, wb, wo)
    return y.reshape(batch, seq_len, D_MODEL)


def kernel(x_prompt, x_sample, norm_gain, w_in, qn_a, kn_a, rel_bias_a, qn_b, kn_b,
           w_branch_a, w_branch_b, w_out):
    depth = norm_gain.shape[0]
    blk = jnp.arange(256) // HEAD_DIM
    bd = jnp.where(blk[:, None] == blk[None, :], 1.0 / HEAD_DIM, 0.0).astype(BF16)
    y_prompt, y_sample = x_prompt, x_sample
    for l in range(depth):
        ng = norm_gain[l].astype(F32)[None, :]
        w_bf = w_in[l].astype(BF16)
        gains = _tile_gains(qn_a[l], kn_a[l], qn_b[l], kn_b[l])
        bias_tab = _na_bias_table(rel_bias_a[l])
        wa = w_branch_a[l].astype(BF16)
        wb = w_branch_b[l].astype(BF16)
        wo = w_out[l].astype(BF16)
        y_prompt = _layer(y_prompt, ng, w_bf, gains, bias_tab, bd, wa, wb, wo)
        y_sample = _layer(y_sample, ng, w_bf, gains, bias_tab, bd, wa, wb, wo)
    return (y_prompt, y_sample)
```

```python
import functools
import math

import jax
import jax.numpy as jnp
from jax import lax
from jax.experimental import pallas as pl
from jax.experimental.pallas import tpu as pltpu

F32 = jnp.float32
BF16 = jnp.bfloat16

D_MODEL = 1024
HEAD_DIM = 64
GRID_W = 64
NA_HEADS = 8
NA_ROWS = 8
NA_COLS = 16
DIL_GROUPS = ((128, 1), (512, 4), (2048, 16))
ROPE_THETA = 10000.0
EPS = 1e-6
NEG = -1e30
LOG2_E = math.log2(math.e)

COL = 512
N_COL_TILES = 18
D_IN = COL * N_COL_TILES
LANES = 128
PAIRS = COL // LANES
HALO = 64
TILE = 2048
BLOCKS = TILE // HALO
TQ = 2 * HALO
ROW_CHUNK = 256

_NORM_TILES = (0, 1)
_PLAIN_TILES = (2, 6)
_DILATED_TILES = {1: (4, 5, 6), 4: (7, 8, 9), 16: (10, 11, 12)}
_SILU_TILES = (3, 13)
_SIGM_TILES = (14, 15, 16, 17)
_Q_TILES = (0, 4, 7, 10)

VMEM_LIMIT = 56 * 1024 * 1024


def _any_of(j, members):
    return functools.reduce(jnp.logical_or, [j == m for m in members])


def _proj_kernel(x_ref, ng_ref, w_ref, gain_ref, cos_ref, sin_ref, bd_ref, o_ref, h_ref, perm_ref):
    j = pl.program_id(1)
    chunks = range(0, TILE, ROW_CHUNK)

    @pl.when(j == 0)
    def _():
        for r0 in chunks:
            x = x_ref[r0:r0 + ROW_CHUNK, :]
            ms = jnp.mean(x * x, axis=-1, keepdims=True)
            h_ref[r0:r0 + ROW_CHUNK, :] = (x * lax.rsqrt(ms + EPS) * ng_ref[...]).astype(BF16)

    def head_norm(a):
        sq = (a * a).astype(BF16)
        bd = bd_ref[...]
        ms = jnp.concatenate(
            [jnp.dot(sq[:, c:c + 256], bd, preferred_element_type=F32) for c in (0, 256)], axis=1)
        return a * lax.rsqrt(ms + EPS) * gain_ref[0]

    def rope(a, r0):
        y = head_norm(a)
        lane = lax.broadcasted_iota(jnp.int32, (1, LANES), 1)
        first_half = (lane % HEAD_DIM) < (HEAD_DIM // 2)
        cos = cos_ref[r0:r0 + ROW_CHUNK, :]
        sin = sin_ref[r0:r0 + ROW_CHUNK, :]
        parts = []
        for c in range(0, COL, LANES):
            yc = y[:, c:c + LANES]
            partner = jnp.where(first_half,
                                pltpu.roll(yc, LANES - HEAD_DIM // 2, 1),
                                pltpu.roll(yc, HEAD_DIM // 2, 1))
            parts.append(yc * cos + partner * sin)
        return jnp.concatenate(parts, axis=1)

    def run(members, epilogue, dil=None):
        @pl.when(_any_of(j, members))
        def _():
            for r0 in chunks:
                a = jnp.dot(h_ref[r0:r0 + ROW_CHUNK, :], w_ref[...], preferred_element_type=F32)
                y = epilogue(a, r0)
                if dil is None:
                    o_ref[r0:r0 + ROW_CHUNK, :] = y.astype(BF16)
                    continue
                for cp in range(PAIRS):
                    perm_ref[cp, r0:r0 + ROW_CHUNK, :] = y[:, cp * LANES:(cp + 1) * LANES]
                span = HALO * dil
                done = r0 + ROW_CHUNK
                for c in range(r0 // span, done // span):
                    for cp in range(PAIRS):
                        for ph in range(dil):
                            blk = c * dil + ph
                            rows = perm_ref[cp, pl.ds(c * span + ph, HALO, stride=dil), :]
                            o_ref[blk * HALO:(blk + 1) * HALO, cp * LANES:(cp + 1) * LANES] = rows.astype(BF16)

    plain = lambda a, r0: a
    run(_NORM_TILES, lambda a, r0: head_norm(a))
    run(_PLAIN_TILES, plain)
    run(_SILU_TILES, lambda a, r0: a * jax.nn.sigmoid(a))
    run(_SIGM_TILES, lambda a, r0: jax.nn.sigmoid(a))
    for dil, (q_tile, k_tile, v_tile) in _DILATED_TILES.items():
        run((q_tile, k_tile), rope, None if dil == 1 else dil)
        if dil != 1:
            run((v_tile,), plain, dil)


def _project(x2d, seq_len, ng, w_bf, gains, cos, sin, bd):
    n_tok = x2d.shape[0]
    pos_blocks = seq_len // TILE
    return pl.pallas_call(
        _proj_kernel,
        grid=(n_tok // TILE, N_COL_TILES),
        in_specs=[
            pl.BlockSpec((TILE, D_MODEL), lambda i, j: (i, 0)),
            pl.BlockSpec((1, D_MODEL), lambda i, j: (0, 0)),
            pl.BlockSpec((D_MODEL, COL), lambda i, j: (0, j)),
            pl.BlockSpec((1, 1, COL), lambda i, j: (j, 0, 0)),
            pl.BlockSpec((TILE, LANES), lambda i, j: (i % pos_blocks, 0)),
            pl.BlockSpec((TILE, LANES), lambda i, j: (i % pos_blocks, 0)),
            pl.BlockSpec((256, 256), lambda i, j: (0, 0)),
        ],
        out_specs=pl.BlockSpec((TILE, COL), lambda i, j: (i, j)),
        out_shape=jax.ShapeDtypeStruct((n_tok, D_IN), BF16),
        scratch_shapes=[pltpu.VMEM((TILE, D_MODEL), BF16), pltpu.VMEM((PAIRS, TILE, LANES), F32)],
        compiler_params=pltpu.CompilerParams(
            dimension_semantics=("parallel", "arbitrary"), vmem_limit_bytes=VMEM_LIMIT),
        name="in_proj",
    )(x2d, ng, w_bf, gains, cos, sin, bd)


def _lane_lo():
    return lax.broadcasted_iota(jnp.int32, (1, LANES), 1) < HEAD_DIM


def _attend_pipelined(units, s_scr, p_scr, ml_scr, lo, need_lse):
    n = len(units)

    def scores(t):
        q, k = units[t][0]()
        zero = jnp.zeros_like(q)
        qq = jnp.concatenate([jnp.where(lo, q, zero), jnp.where(lo, zero, q)], axis=0)
        s = lax.dot_general(qq, k, (((1,), (1,)), ((), ())), preferred_element_type=F32) + units[t][1]()
        s_scr[t % 2] = s
        ml_scr[t % 2, 2] = jnp.broadcast_to(jnp.max(s, axis=-1, keepdims=True), (s.shape[0], LANES))

    def numerator(t):
        m = ml_scr[t % 2, 2]
        partial = None
        for c in range(0, s_scr.shape[2], LANES):
            p = jnp.exp2(s_scr[t % 2, :, c:c + LANES] - m)
            p_scr[t % 2, :, c:c + LANES] = p.astype(BF16)
            partial = p if partial is None else partial + p
        l = jnp.broadcast_to(jnp.sum(partial, axis=-1, keepdims=True), m.shape)
        ml_scr[t % 2, 0] = l
        if need_lse:
            ml_scr[t % 2, 1] = m + jnp.log2(l)

    def values(t):
        v = units[t][2]()
        o2 = jnp.dot(p_scr[t % 2], v, preferred_element_type=F32) / ml_scr[t % 2, 0]
        nq = o2.shape[0] // 2
        lse = None
        if need_lse:
            lse2 = ml_scr[t % 2, 1]
            lse = jnp.where(lo, lse2[:nq], lse2[nq:])
        units[t][3](jnp.where(lo, o2[:nq], o2[nq:]), lse)

    for t in range(n + 2):
        if t < n:
            scores(t)
        if 1 <= t <= n:
            numerator(t - 1)
        if t >= 2:
            values(t - 2)


def _attend_scratch(rows, keys):
    return [pltpu.VMEM((2, rows, keys), F32), pltpu.VMEM((2, rows, keys), BF16),
            pltpu.VMEM((2, 3, rows, LANES), F32)]


def _na_kernel(q_ref, k_ref, v_ref, g_ref, bias_ref, o_ref, s_scr, p_scr, ml_scr, *, rows, rq):
    rb = pl.program_id(2)
    lo = _lane_lo()
    kwin = NA_ROWS * GRID_W

    def unit(rr):
        r = rb * rq + rr
        rs = jnp.clip(r - NA_ROWS // 2, 0, rows - NA_ROWS)
        shift = rs - r + (NA_ROWS - 1)
        window = pl.ds(pl.multiple_of(rs * GRID_W, GRID_W), kwin)
        rows_q = slice(rr * GRID_W, (rr + 1) * GRID_W)

        def finish(o, lse):
            o_ref[rows_q, :] = (o * g_ref[rows_q, :].astype(F32)).astype(BF16)

        return (lambda: (q_ref[rows_q, :], k_ref[window, :]),
                lambda: bias_ref[shift],
                lambda: v_ref[window, :],
                finish)

    _attend_pipelined([unit(rr) for rr in range(rq)], s_scr, p_scr, ml_scr, lo, need_lse=False)


def _neighbourhood(proj3, bias_tab, rq=16):
    batch, seq_len, _ = proj3.shape
    rows = seq_len // GRID_W
    tq = rq * GRID_W
    return pl.pallas_call(
        functools.partial(_na_kernel, rows=rows, rq=rq),
        grid=(batch, PAIRS, rows // rq),
        in_specs=[
            pl.BlockSpec((None, tq, LANES), lambda b, hp, rb: (b, rb, 0 * PAIRS + hp)),
            pl.BlockSpec((None, seq_len, LANES), lambda b, hp, rb: (b, 0, 1 * PAIRS + hp)),
            pl.BlockSpec((None, seq_len, LANES), lambda b, hp, rb: (b, 0, 2 * PAIRS + hp)),
            pl.BlockSpec((None, tq, LANES), lambda b, hp, rb: (b, rb, 3 * PAIRS + hp)),
            pl.BlockSpec((None, NA_ROWS, 2 * GRID_W, NA_ROWS * GRID_W), lambda b, hp, rb: (hp, 0, 0, 0)),
        ],
        out_specs=pl.BlockSpec((None, tq, LANES), lambda b, hp, rb: (b, rb, hp)),
        out_shape=jax.ShapeDtypeStruct((batch, seq_len, COL), BF16),
        scratch_shapes=_attend_scratch(2 * GRID_W, NA_ROWS * GRID_W),
        compiler_params=pltpu.CompilerParams(
            dimension_semantics=("parallel", "parallel", "arbitrary"), vmem_limit_bytes=VMEM_LIMIT),
        name="na_attn",
    )(proj3, proj3, proj3, proj3, bias_tab)


def _na_bias_table(rel_bias):
    heads, n_dr, n_dc = rel_bias.shape
    pad = GRID_W - NA_COLS
    padded = jnp.pad(rel_bias.astype(F32), ((0, 0), (0, 0), (pad, pad)))
    toeplitz = jnp.stack([padded[:, :, GRID_W - 1 - qc:2 * GRID_W - 1 - qc] for qc in range(GRID_W)], axis=2)
    c = jnp.arange(GRID_W)
    cs = jnp.clip(c - NA_COLS // 2, 0, GRID_W - NA_COLS)
    ok = (c[None, :] >= cs[:, None]) & (c[None, :] < cs[:, None] + NA_COLS)
    toeplitz = jnp.where(ok[None, None], toeplitz * LOG2_E, NEG)
    tab = jnp.stack([toeplitz[:, sh:sh + NA_ROWS] for sh in range(NA_ROWS)], axis=1)
    tab = tab.transpose(0, 1, 3, 2, 4).reshape(heads // 2, 2, NA_ROWS, GRID_W, NA_ROWS * GRID_W)
    return tab.transpose(0, 2, 1, 3, 4).reshape(heads // 2, NA_ROWS, 2 * GRID_W, NA_ROWS * GRID_W)


def _band_bias(first, last):
    nk = TQ + 2 * HALO
    qpos = lax.broadcasted_iota(jnp.int32, (2 * TQ, nk), 0) % TQ
    col = lax.broadcasted_iota(jnp.int32, (2 * TQ, nk), 1)
    bias = jnp.where(jnp.abs(col - HALO - qpos) <= HALO, 0.0, NEG).astype(F32)
    if first is not None:
        bias = bias + jnp.where(col < HALO, jnp.where(first, NEG, 0.0).astype(F32), 0.0)
    if last is not None:
        bias = bias + jnp.where(col >= TQ + HALO, jnp.where(last, NEG, 0.0).astype(F32), 0.0)
    return bias


def _band_dilated_kernel(q_ref, kp_ref, kc_ref, kn_ref, vp_ref, vc_ref, vn_ref, out_ref, lse_ref,
                         o_scr, l_scr, bias_scr, s_scr, p_scr, ml_scr, *, dil, tiles):
    i = pl.program_id(1)
    lo = _lane_lo()
    n_units = BLOCKS // dil // 2
    first, last = i == 0, i == tiles - 1
    bias_slot = {}
    for a in range(n_units):
        key = (a == 0, a == n_units - 1)
        if key not in bias_slot:
            bias_slot[key] = len(bias_slot)
            bias_scr[bias_slot[key]] = _band_bias(first if key[0] else None, last if key[1] else None)

    def rows(chunk, ph):
        return slice((chunk * dil + ph) * HALO, (chunk * dil + ph + 1) * HALO)

    def unit(ph, a, hp):
        sl = slice(hp * LANES, (hp + 1) * LANES)
        halo = slice(ph * HALO, (ph + 1) * HALO)
        slot = bias_slot[(a == 0, a == n_units - 1)]

        def window(prev_ref, cur_ref, next_ref):
            head = prev_ref[halo, sl] if a == 0 else cur_ref[rows(2 * a - 1, ph), sl]
            tail = next_ref[halo, sl] if a == n_units - 1 else cur_ref[rows(2 * a + 2, ph), sl]
            return jnp.concatenate(
                [head, cur_ref[rows(2 * a, ph), sl], cur_ref[rows(2 * a + 1, ph), sl], tail], axis=0)

        def load_qk():
            q = jnp.concatenate([q_ref[rows(2 * a, ph), sl], q_ref[rows(2 * a + 1, ph), sl]], axis=0)
            return q, window(kp_ref, kc_ref, kn_ref)

        def finish(o, lse):
            tok = pl.ds(a * TQ * dil + ph, TQ, stride=dil)
            o_scr[hp, tok, :] = o
            l_scr[hp, tok, :] = lse

        return load_qk, lambda: bias_scr[slot], lambda: window(vp_ref, vc_ref, vn_ref), finish

    units = [unit(ph, a, hp) for ph in range(dil) for a in range(n_units) for hp in range(PAIRS)]
    _attend_pipelined(units, s_scr, p_scr, ml_scr, lo, need_lse=True)
    for hp in range(PAIRS):
        out_ref[:, hp * LANES:(hp + 1) * LANES] = o_scr[hp].astype(BF16)
        lse_ref[:, hp * LANES:(hp + 1) * LANES] = l_scr[hp]


def _band_dilated(proj3, group, dil):
    batch, seq_len, _ = proj3.shape
    tiles = seq_len // TILE
    cq, ck, cv = 4 + 3 * group, 5 + 3 * group, 6 + 3 * group
    halo_rows = dil * HALO
    per_tile = TILE // halo_rows
    last_unit = seq_len // halo_rows - 1

    def cur(col):
        return pl.BlockSpec((None, TILE, COL), lambda b, i: (b, i, col))

    def prev(col):
        return pl.BlockSpec((None, halo_rows, COL), lambda b, i: (b, jnp.maximum(i * per_tile - 1, 0), col))

    def nxt(col):
        return pl.BlockSpec((None, halo_rows, COL),
                            lambda b, i: (b, jnp.minimum((i + 1) * per_tile, last_unit), col))

    tok_spec = pl.BlockSpec((None, TILE, COL), lambda b, i: (b, i, 0))
    return pl.pallas_call(
        functools.partial(_band_dilated_kernel, dil=dil, tiles=tiles),
        grid=(batch, tiles),
        in_specs=[cur(cq), prev(ck), cur(ck), nxt(ck), prev(cv), cur(cv), nxt(cv)],
        out_specs=[tok_spec, tok_spec],
        out_shape=[jax.ShapeDtypeStruct((batch, seq_len, COL), BF16),
                   jax.ShapeDtypeStruct((batch, seq_len, COL), F32)],
        scratch_shapes=[pltpu.VMEM((PAIRS, TILE, LANES), F32), pltpu.VMEM((PAIRS, TILE, LANES), F32),
                        pltpu.VMEM((3, 2 * TQ, TQ + 2 * HALO), F32)] + _attend_scratch(2 * TQ, TQ + 2 * HALO),
        compiler_params=pltpu.CompilerParams(
            dimension_semantics=("parallel", "arbitrary"), vmem_limit_bytes=VMEM_LIMIT),
        name="band_d%d" % dil,
    )(*([proj3] * 7))


def _band_merge_kernel(q_ref, kp_ref, kc_ref, kn_ref, vp_ref, vc_ref, vn_ref,
                       o1_ref, l1_ref, o2_ref, l2_ref, g_ref, out_ref, bias_scr, s_scr, p_scr, ml_scr,
                       *, tm, tiles):
    i = pl.program_id(1)
    lo = _lane_lo()
    n_units = tm // TQ
    bias_scr[0] = _band_bias(i == 0, None)
    bias_scr[1] = _band_bias(None, None)
    bias_scr[2] = _band_bias(None, i == tiles - 1)

    def unit(a, hp):
        sl = slice(hp * LANES, (hp + 1) * LANES)
        qrows = slice(a * TQ, (a + 1) * TQ)
        slot = 0 if a == 0 else (2 if a == n_units - 1 else 1)

        def window(prev_ref, cur_ref, next_ref):
            if a == 0:
                return jnp.concatenate([prev_ref[:, sl], cur_ref[0:TQ + HALO, sl]], axis=0)
            if a == n_units - 1:
                return jnp.concatenate([cur_ref[tm - TQ - HALO:tm, sl], next_ref[:, sl]], axis=0)
            return cur_ref[a * TQ - HALO:(a + 1) * TQ + HALO, sl]

        def finish(o, lse):
            l1 = l1_ref[qrows, sl]
            l2 = l2_ref[qrows, sl]
            top = jnp.maximum(lse, jnp.maximum(l1, l2))
            w0 = jnp.exp2(lse - top)
            w1 = jnp.exp2(l1 - top)
            w2 = jnp.exp2(l2 - top)
            num = w0 * o + w1 * o1_ref[qrows, sl].astype(F32) + w2 * o2_ref[qrows, sl].astype(F32)
            ob = num / (w0 + w1 + w2)
            out_ref[qrows, sl] = (ob * g_ref[qrows, sl].astype(F32)).astype(BF16)

        return (lambda: (q_ref[qrows, sl], window(kp_ref, kc_ref, kn_ref)),
                lambda: bias_scr[slot],
                lambda: window(vp_ref, vc_ref, vn_ref),
                finish)

    units = [unit(a, hp) for a in range(n_units) for hp in range(PAIRS)]
    _attend_pipelined(units, s_scr, p_scr, ml_scr, lo, need_lse=True)


def _band_merge(proj3, o1, l1, o2, l2, tm=1024):
    batch, seq_len, _ = proj3.shape
    tiles = seq_len // tm
    hb = tm // HALO
    last_halo = seq_len // HALO - 1
    cq, ck, cv, cg = 4, 5, 6, 13

    def cur(col):
        return pl.BlockSpec((None, tm, COL), lambda b, i: (b, i, col))

    def prev(col):
        return pl.BlockSpec((None, HALO, COL), lambda b, i: (b, jnp.maximum(i * hb - 1, 0), col))

    def nxt(col):
        return pl.BlockSpec((None, HALO, COL), lambda b, i: (b, jnp.minimum((i + 1) * hb, last_halo), col))

    tok_spec = pl.BlockSpec((None, tm, COL), lambda b, i: (b, i, 0))
    return pl.pallas_call(
        functools.partial(_band_merge_kernel, tm=tm, tiles=tiles),
        grid=(batch, tiles),
        in_specs=[cur(cq), prev(ck), cur(ck), nxt(ck), prev(cv), cur(cv), nxt(cv)] + [tok_spec] * 4 + [cur(cg)],
        out_specs=tok_spec,
        out_shape=jax.ShapeDtypeStruct((batch, seq_len, COL), BF16),
        scratch_shapes=[pltpu.VMEM((3, 2 * TQ, TQ + 2 * HALO), F32)] + _attend_scratch(2 * TQ, TQ + 2 * HALO),
        compiler_params=pltpu.CompilerParams(
            dimension_semantics=("parallel", "arbitrary"), vmem_limit_bytes=VMEM_LIMIT),
        name="band_merge",
    )(*([proj3] * 7), o1, l1, o2, l2, proj3)


def _out_kernel(x_ref, ua_ref, ub_ref, sa_ref, sb_ref, wa_ref, wb_ref, wo_ref, y_ref):
    br_a = jnp.dot(ua_ref[...], wa_ref[...], preferred_element_type=F32)
    br_b = jnp.dot(ub_ref[...], wb_ref[...], preferred_element_type=F32)
    merged = sa_ref[...].astype(F32) * br_a + sb_ref[...].astype(F32) * br_b
    y_ref[...] = x_ref[...] + jnp.dot(merged.astype(BF16), wo_ref[...], preferred_element_type=F32)


def _output(x2d, ua, ub, proj, wa, wb, wo, tm=512):
    n_tok = x2d.shape[0]
    sig_a = (14 * COL) // D_MODEL
    sig_b = (16 * COL) // D_MODEL
    return pl.pallas_call(
        _out_kernel,
        grid=(n_tok // tm,),
        in_specs=[
            pl.BlockSpec((tm, D_MODEL), lambda i: (i, 0)),
            pl.BlockSpec((tm, COL), lambda i: (i, 0)),
            pl.BlockSpec((tm, COL), lambda i: (i, 0)),
            pl.BlockSpec((tm, D_MODEL), lambda i: (i, sig_a)),
            pl.BlockSpec((tm, D_MODEL), lambda i: (i, sig_b)),
            pl.BlockSpec((COL, D_MODEL), lambda i: (0, 0)),
            pl.BlockSpec((COL, D_MODEL), lambda i: (0, 0)),
            pl.BlockSpec((D_MODEL, D_MODEL), lambda i: (0, 0)),
        ],
        out_specs=pl.BlockSpec((tm, D_MODEL), lambda i: (i, 0)),
        out_shape=jax.ShapeDtypeStruct((n_tok, D_MODEL), F32),
        compiler_params=pltpu.CompilerParams(
            dimension_semantics=("parallel",), vmem_limit_bytes=VMEM_LIMIT),
        name="out_proj",
    )(x2d, ua, ub, proj, proj, wa, wb, wo)


def _rope_tables(seq_len):
    inv = ROPE_THETA ** (-jnp.arange(0, HEAD_DIM, 2, dtype=F32) / HEAD_DIM)
    ang = jnp.arange(seq_len, dtype=F32)[:, None] * inv[None, :]
    cos = jnp.cos(ang)
    sin = jnp.sin(ang)
    reps = LANES // HEAD_DIM
    cos_t = jnp.tile(jnp.concatenate([cos, cos], axis=1), (1, reps))
    sin_t = jnp.tile(jnp.concatenate([-sin, sin], axis=1), (1, reps))
    return cos_t, sin_t


def _tile_gains(qn_a, kn_a, qn_b, kn_b):
    scale = LOG2_E / math.sqrt(HEAD_DIM)
    heads = COL // HEAD_DIM
    rows = []
    for t in range(N_COL_TILES):
        if t == 0:
            g = qn_a
        elif t == 1:
            g = kn_a
        elif t in (4, 7, 10):
            g = qn_b
        elif t in (5, 8, 11):
            g = kn_b
        else:
            g = jnp.ones((HEAD_DIM,), F32)
        g = g.astype(F32)
        if t in _Q_TILES:
            g = g * scale
        rows.append(jnp.tile(g, heads))
    return jnp.stack(rows)[:, None, :]


def _layer(x, ng, w_bf, gains, bias_tab, bd, wa, wb, wo):
    batch, seq_len, _ = x.shape
    assert seq_len % TILE == 0
    x2d = x.reshape(batch * seq_len, D_MODEL)
    cos, sin = _rope_tables(seq_len)
    proj = _project(x2d, seq_len, ng, w_bf, gains, cos, sin, bd)
    proj3 = proj.reshape(batch, seq_len, D_IN)
    ua = _neighbourhood(proj3, bias_tab)
    o2, l2 = _band_dilated(proj3, 2, DIL_GROUPS[2][1])
    o1, l1 = _band_dilated(proj3, 1, DIL_GROUPS[1][1])
    ub = _band_merge(proj3, o1, l1, o2, l2)
    y = _output(x2d, ua.reshape(-1, COL), ub.reshape(-1, COL), proj, wa, wb, wo)
    return y.reshape(batch, seq_len, D_MODEL)


def kernel(x_prompt, x_sample, norm_gain, w_in, qn_a, kn_a, rel_bias_a, qn_b, kn_b,
           w_branch_a, w_branch_b, w_out):
    depth = norm_gain.shape[0]
    blk = jnp.arange(256) // HEAD_DIM
    bd = jnp.where(blk[:, None] == blk[None, :], 1.0 / HEAD_DIM, 0.0).astype(BF16)
    y_prompt, y_sample = x_prompt, x_sample
    for l in range(depth):
        ng = norm_gain[l].astype(F32)[None, :]
        w_bf = w_in[l].astype(BF16)
        gains = _tile_gains(qn_a[l], kn_a[l], qn_b[l], kn_b[l])
        bias_tab = _na_bias_table(rel_bias_a[l])
        wa = w_branch_a[l].astype(BF16)
        wb = w_branch_b[l].astype(BF16)
        wo = w_out[l].astype(BF16)
        y_prompt = _layer(y_prompt, ng, w_bf, gains, bias_tab, bd, wa, wb, wo)
        y_sample = _layer(y_sample, ng, w_bf, gains, bias_tab, bd, wa, wb, wo)
    return (y_prompt, y_sample)
```

```python
import functools
import math

import jax
import jax.numpy as jnp
from jax import lax
from jax.experimental import pallas as pl
from jax.experimental.pallas import tpu as pltpu

F32 = jnp.float32
BF16 = jnp.bfloat16

D_MODEL = 1024
HEAD_DIM = 64
GRID_W = 64
NA_HEADS = 8
NA_ROWS = 8
NA_COLS = 16
DIL_GROUPS = ((128, 1), (512, 4), (2048, 16))
ROPE_THETA = 10000.0
EPS = 1e-6
NEG = -1e30
LOG2_E = math.log2(math.e)

COL = 512
N_COL_TILES = 18
D_IN = COL * N_COL_TILES
LANES = 128
PAIRS = COL // LANES
HALO = 64
TILE = 2048
BLOCKS = TILE // HALO
TQ = 2 * HALO
NORM_CHUNK = 512
PLAIN_CHUNK = 256

_NORM_TILES = (0, 1)
_PLAIN_TILES = (2, 6)
_DILATED_TILES = {1: (4, 5, 6), 4: (7, 8, 9), 16: (10, 11, 12)}
_SILU_TILES = (3, 13)
_SIGM_TILES = (14, 15, 16, 17)
_Q_TILES = (0, 4, 7, 10)

VMEM_LIMIT = 56 * 1024 * 1024


def _any_of(j, members):
    return functools.reduce(jnp.logical_or, [j == m for m in members])


def _proj_kernel(x_ref, ng_ref, w_ref, gain_ref, cos_ref, sin_ref, bd_ref, o_ref, h_ref, perm_ref):
    j = pl.program_id(1)
    lane = lax.broadcasted_iota(jnp.int32, (1, LANES), 1)
    first_half = (lane % HEAD_DIM) < (HEAD_DIM // 2)

    def normalize_rows(r0, rc):
        x = x_ref[r0:r0 + rc, :]
        ms = jnp.mean(x * x, axis=-1, keepdims=True)
        h_ref[r0:r0 + rc, :] = (x * lax.rsqrt(ms + EPS) * ng_ref[...]).astype(BF16)

    def head_norm(a, r0, rc):
        sq = (a * a).astype(BF16)
        bd = bd_ref[...]
        ms = jnp.concatenate(
            [jnp.dot(sq[:, c:c + 256], bd, preferred_element_type=F32) for c in (0, 256)], axis=1)
        return a * lax.rsqrt(ms + EPS) * gain_ref[0]

    def rope(a, r0, rc):
        y = head_norm(a, r0, rc)
        cos = cos_ref[r0:r0 + rc, :]
        sin = sin_ref[r0:r0 + rc, :]
        parts = []
        for c in range(0, COL, LANES):
            yc = y[:, c:c + LANES]
            partner = jnp.where(first_half,
                                pltpu.roll(yc, LANES - HEAD_DIM // 2, 1),
                                pltpu.roll(yc, HEAD_DIM // 2, 1))
            parts.append(yc * cos + partner * sin)
        return jnp.concatenate(parts, axis=1)

    def sigmoid(a):
        return 0.5 * jnp.tanh(0.5 * a) + 0.5

    def run(members, epilogue, rc, dil=None, first=False):
        @pl.when(_any_of(j, members))
        def _():
            for r0 in range(0, TILE, rc):
                if first:
                    normalize_rows(r0, rc)
                a = jnp.dot(h_ref[r0:r0 + rc, :], w_ref[...], preferred_element_type=F32)
                y = epilogue(a, r0, rc)
                if dil is None:
                    o_ref[r0:r0 + rc, :] = y.astype(BF16)
                    continue
                for cp in range(PAIRS):
                    perm_ref[cp, r0:r0 + rc, :] = y[:, cp * LANES:(cp + 1) * LANES]
                span = HALO * dil
                done = r0 + rc
                for c in range(r0 // span, done // span):
                    for cp in range(PAIRS):
                        for ph in range(dil):
                            blk = c * dil + ph
                            rows = perm_ref[cp, pl.ds(c * span + ph, HALO, stride=dil), :]
                            o_ref[blk * HALO:(blk + 1) * HALO, cp * LANES:(cp + 1) * LANES] = rows.astype(BF16)

    plain = lambda a, r0, rc: a
    assert _NORM_TILES[0] == 0
    run(_NORM_TILES[:1], head_norm, NORM_CHUNK, first=True)
    run(_NORM_TILES[1:], head_norm, NORM_CHUNK)
    run(_PLAIN_TILES, plain, PLAIN_CHUNK)
    run(_SILU_TILES, lambda a, r0, rc: a * sigmoid(a), PLAIN_CHUNK)
    run(_SIGM_TILES, lambda a, r0, rc: sigmoid(a), PLAIN_CHUNK)
    for dil, (q_tile, k_tile, v_tile) in _DILATED_TILES.items():
        run((q_tile, k_tile), rope, NORM_CHUNK, None if dil == 1 else dil)
        if dil != 1:
            run((v_tile,), plain, PLAIN_CHUNK, dil)


def _project(x2d, seq_len, ng, w_bf, gains, cos, sin, bd):
    n_tok = x2d.shape[0]
    pos_blocks = seq_len // TILE
    return pl.pallas_call(
        _proj_kernel,
        grid=(n_tok // TILE, N_COL_TILES),
        in_specs=[
            pl.BlockSpec((TILE, D_MODEL), lambda i, j: (i, 0)),
            pl.BlockSpec((1, D_MODEL), lambda i, j: (0, 0)),
            pl.BlockSpec((D_MODEL, COL), lambda i, j: (0, j)),
            pl.BlockSpec((1, 1, COL), lambda i, j: (j, 0, 0)),
            pl.BlockSpec((TILE, LANES), lambda i, j: (i % pos_blocks, 0)),
            pl.BlockSpec((TILE, LANES), lambda i, j: (i % pos_blocks, 0)),
            pl.BlockSpec((256, 256), lambda i, j: (0, 0)),
        ],
        out_specs=pl.BlockSpec((TILE, COL), lambda i, j: (i, j)),
        out_shape=jax.ShapeDtypeStruct((n_tok, D_IN), BF16),
        scratch_shapes=[pltpu.VMEM((TILE, D_MODEL), BF16), pltpu.VMEM((PAIRS, TILE, LANES), F32)],
        compiler_params=pltpu.CompilerParams(
            dimension_semantics=("parallel", "arbitrary"), vmem_limit_bytes=VMEM_LIMIT),
        name="in_proj",
    )(x2d, ng, w_bf, gains, cos, sin, bd)


def _lane_lo():
    return lax.broadcasted_iota(jnp.int32, (1, LANES), 1) < HEAD_DIM


def _attend_pipelined(units, s_scr, p_scr, ml_scr, lo, need_lse):
    n = len(units)

    def scores(t):
        q, k = units[t][0]()
        zero = jnp.zeros_like(q)
        qq = jnp.concatenate([jnp.where(lo, q, zero), jnp.where(lo, zero, q)], axis=0)
        s = lax.dot_general(qq, k, (((1,), (1,)), ((), ())), preferred_element_type=F32) + units[t][1]()
        s_scr[t % 2] = s
        ml_scr[t % 2, 2] = jnp.broadcast_to(jnp.max(s, axis=-1, keepdims=True), (s.shape[0], LANES))

    def numerator(t):
        m = ml_scr[t % 2, 2]
        partial = None
        for c in range(0, s_scr.shape[2], LANES):
            p = jnp.exp2(s_scr[t % 2, :, c:c + LANES] - m)
            p_scr[t % 2, :, c:c + LANES] = p.astype(BF16)
            partial = p if partial is None else partial + p
        l = jnp.broadcast_to(jnp.sum(partial, axis=-1, keepdims=True), m.shape)
        ml_scr[t % 2, 0] = l
        if need_lse:
            ml_scr[t % 2, 1] = m + jnp.log2(l)

    def values(t):
        v = units[t][2]()
        o2 = jnp.dot(p_scr[t % 2], v, preferred_element_type=F32) / ml_scr[t % 2, 0]
        nq = o2.shape[0] // 2
        lse = None
        if need_lse:
            lse2 = ml_scr[t % 2, 1]
            lse = jnp.where(lo, lse2[:nq], lse2[nq:])
        units[t][3](jnp.where(lo, o2[:nq], o2[nq:]), lse)

    for t in range(n + 2):
        if t < n:
            scores(t)
        if 1 <= t <= n:
            numerator(t - 1)
        if t >= 2:
            values(t - 2)


def _attend_scratch(rows, keys):
    return [pltpu.VMEM((2, rows, keys), F32), pltpu.VMEM((2, rows, keys), BF16),
            pltpu.VMEM((2, 3, rows, LANES), F32)]


def _na_kernel(q_ref, k_ref, v_ref, g_ref, bias_ref, o_ref, s_scr, p_scr, ml_scr, *, rows, rq):
    rb = pl.program_id(2)
    lo = _lane_lo()
    kwin = NA_ROWS * GRID_W

    def unit(rr):
        r = rb * rq + rr
        rs = jnp.clip(r - NA_ROWS // 2, 0, rows - NA_ROWS)
        shift = rs - r + (NA_ROWS - 1)
        window = pl.ds(pl.multiple_of(rs * GRID_W, GRID_W), kwin)
        rows_q = slice(rr * GRID_W, (rr + 1) * GRID_W)

        def finish(o, lse):
            o_ref[rows_q, :] = (o * g_ref[rows_q, :].astype(F32)).astype(BF16)

        return (lambda: (q_ref[rows_q, :], k_ref[window, :]),
                lambda: bias_ref[shift],
                lambda: v_ref[window, :],
                finish)

    _attend_pipelined([unit(rr) for rr in range(rq)], s_scr, p_scr, ml_scr, lo, need_lse=False)


def _neighbourhood(proj3, bias_tab, rq=32):
    batch, seq_len, _ = proj3.shape
    rows = seq_len // GRID_W
    tq = rq * GRID_W
    return pl.pallas_call(
        functools.partial(_na_kernel, rows=rows, rq=rq),
        grid=(batch, PAIRS, rows // rq),
        in_specs=[
            pl.BlockSpec((None, tq, LANES), lambda b, hp, rb: (b, rb, 0 * PAIRS + hp)),
            pl.BlockSpec((None, seq_len, LANES), lambda b, hp, rb: (b, 0, 1 * PAIRS + hp)),
            pl.BlockSpec((None, seq_len, LANES), lambda b, hp, rb: (b, 0, 2 * PAIRS + hp)),
            pl.BlockSpec((None, tq, LANES), lambda b, hp, rb: (b, rb, 3 * PAIRS + hp)),
            pl.BlockSpec((None, NA_ROWS, 2 * GRID_W, NA_ROWS * GRID_W), lambda b, hp, rb: (hp, 0, 0, 0)),
        ],
        out_specs=pl.BlockSpec((None, tq, LANES), lambda b, hp, rb: (b, rb, hp)),
        out_shape=jax.ShapeDtypeStruct((batch, seq_len, COL), BF16),
        scratch_shapes=_attend_scratch(2 * GRID_W, NA_ROWS * GRID_W),
        compiler_params=pltpu.CompilerParams(
            dimension_semantics=("parallel", "parallel", "arbitrary"), vmem_limit_bytes=VMEM_LIMIT),
        name="na_attn",
    )(proj3, proj3, proj3, proj3, bias_tab)


def _na_bias_table(rel_bias):
    heads, n_dr, n_dc = rel_bias.shape
    period = 2 * GRID_W
    diag = jnp.pad(rel_bias.astype(F32), ((0, 0), (0, 0), (0, period - n_dc)))
    diag = jnp.roll(diag, -(NA_COLS - 1), axis=-1)
    tiled = jnp.tile(diag, (1, 1, GRID_W))[:, :, :GRID_W * (period - 1)]
    toeplitz = tiled.reshape(heads, n_dr, GRID_W, period - 1)[..., :GRID_W]
    c = jnp.arange(GRID_W)
    cs = jnp.clip(c - NA_COLS // 2, 0, GRID_W - NA_COLS)
    ok = (c[None, :] >= cs[:, None]) & (c[None, :] < cs[:, None] + NA_COLS)
    toeplitz = jnp.where(ok[None, None], toeplitz * LOG2_E, NEG)
    tab = jnp.stack([toeplitz[:, sh:sh + NA_ROWS] for sh in range(NA_ROWS)], axis=1)
    tab = tab.transpose(0, 1, 3, 2, 4).reshape(heads // 2, 2, NA_ROWS, GRID_W, NA_ROWS * GRID_W)
    return tab.transpose(0, 2, 1, 3, 4).reshape(heads // 2, NA_ROWS, 2 * GRID_W, NA_ROWS * GRID_W)


def _band_bias(first, last):
    nk = TQ + 2 * HALO
    qpos = lax.broadcasted_iota(jnp.int32, (2 * TQ, nk), 0) % TQ
    col = lax.broadcasted_iota(jnp.int32, (2 * TQ, nk), 1)
    bias = jnp.where(jnp.abs(col - HALO - qpos) <= HALO, 0.0, NEG).astype(F32)
    if first is not None:
        bias = bias + jnp.where(col < HALO, jnp.where(first, NEG, 0.0).astype(F32), 0.0)
    if last is not None:
        bias = bias + jnp.where(col >= TQ + HALO, jnp.where(last, NEG, 0.0).astype(F32), 0.0)
    return bias


def _band_dilated_kernel(q_ref, kp_ref, kc_ref, kn_ref, vp_ref, vc_ref, vn_ref, out_ref, lse_ref,
                         o_scr, l_scr, bias_scr, s_scr, p_scr, ml_scr, *, dil, tiles):
    i = pl.program_id(1)
    lo = _lane_lo()
    n_units = BLOCKS // dil // 2
    first, last = i == 0, i == tiles - 1
    bias_slot = {}
    for a in range(n_units):
        key = (a == 0, a == n_units - 1)
        if key not in bias_slot:
            bias_slot[key] = len(bias_slot)
            bias_scr[bias_slot[key]] = _band_bias(first if key[0] else None, last if key[1] else None)

    def rows(chunk, ph):
        return slice((chunk * dil + ph) * HALO, (chunk * dil + ph + 1) * HALO)

    def unit(ph, a, hp):
        sl = slice(hp * LANES, (hp + 1) * LANES)
        halo = slice(ph * HALO, (ph + 1) * HALO)
        slot = bias_slot[(a == 0, a == n_units - 1)]

        def window(prev_ref, cur_ref, next_ref):
            head = prev_ref[halo, sl] if a == 0 else cur_ref[rows(2 * a - 1, ph), sl]
            tail = next_ref[halo, sl] if a == n_units - 1 else cur_ref[rows(2 * a + 2, ph), sl]
            return jnp.concatenate(
                [head, cur_ref[rows(2 * a, ph), sl], cur_ref[rows(2 * a + 1, ph), sl], tail], axis=0)

        def load_qk():
            q = jnp.concatenate([q_ref[rows(2 * a, ph), sl], q_ref[rows(2 * a + 1, ph), sl]], axis=0)
            return q, window(kp_ref, kc_ref, kn_ref)

        def finish(o, lse):
            tok = pl.ds(a * TQ * dil + ph, TQ, stride=dil)
            o_scr[hp, tok, :] = o
            l_scr[hp, tok, :] = lse

        return load_qk, lambda: bias_scr[slot], lambda: window(vp_ref, vc_ref, vn_ref), finish

    units = [unit(ph, a, hp) for ph in range(dil) for a in range(n_units) for hp in range(PAIRS)]
    _attend_pipelined(units, s_scr, p_scr, ml_scr, lo, need_lse=True)
    for hp in range(PAIRS):
        out_ref[:, hp * LANES:(hp + 1) * LANES] = o_scr[hp].astype(BF16)
        lse_ref[:, hp * LANES:(hp + 1) * LANES] = l_scr[hp]


def _band_dilated(proj3, group, dil):
    batch, seq_len, _ = proj3.shape
    tiles = seq_len // TILE
    cq, ck, cv = 4 + 3 * group, 5 + 3 * group, 6 + 3 * group
    halo_rows = dil * HALO
    per_tile = TILE // halo_rows
    last_unit = seq_len // halo_rows - 1

    def cur(col):
        return pl.BlockSpec((None, TILE, COL), lambda b, i: (b, i, col))

    def prev(col):
        return pl.BlockSpec((None, halo_rows, COL), lambda b, i: (b, jnp.maximum(i * per_tile - 1, 0), col))

    def nxt(col):
        return pl.BlockSpec((None, halo_rows, COL),
                            lambda b, i: (b, jnp.minimum((i + 1) * per_tile, last_unit), col))

    tok_spec = pl.BlockSpec((None, TILE, COL), lambda b, i: (b, i, 0))
    return pl.pallas_call(
        functools.partial(_band_dilated_kernel, dil=dil, tiles=tiles),
        grid=(batch, tiles),
        in_specs=[cur(cq), prev(ck), cur(ck), nxt(ck), prev(cv), cur(cv), nxt(cv)],
        out_specs=[tok_spec, tok_spec],
        out_shape=[jax.ShapeDtypeStruct((batch, seq_len, COL), BF16),
                   jax.ShapeDtypeStruct((batch, seq_len, COL), F32)],
        scratch_shapes=[pltpu.VMEM((PAIRS, TILE, LANES), F32), pltpu.VMEM((PAIRS, TILE, LANES), F32),
                        pltpu.VMEM((3, 2 * TQ, TQ + 2 * HALO), F32)] + _attend_scratch(2 * TQ, TQ + 2 * HALO),
        compiler_params=pltpu.CompilerParams(
            dimension_semantics=("parallel", "arbitrary"), vmem_limit_bytes=VMEM_LIMIT),
        name="band_d%d" % dil,
    )(*([proj3] * 7))


def _band_merge_kernel(q_ref, kp_ref, kc_ref, kn_ref, vp_ref, vc_ref, vn_ref,
                       o1_ref, l1_ref, o2_ref, l2_ref, g_ref, out_ref, bias_scr, s_scr, p_scr, ml_scr,
                       *, tm, tiles):
    i = pl.program_id(1)
    lo = _lane_lo()
    n_units = tm // TQ
    bias_scr[0] = _band_bias(i == 0, None)
    bias_scr[1] = _band_bias(None, None)
    bias_scr[2] = _band_bias(None, i == tiles - 1)

    def unit(a, hp):
        sl = slice(hp * LANES, (hp + 1) * LANES)
        qrows = slice(a * TQ, (a + 1) * TQ)
        slot = 0 if a == 0 else (2 if a == n_units - 1 else 1)

        def window(prev_ref, cur_ref, next_ref):
            if a == 0:
                return jnp.concatenate([prev_ref[:, sl], cur_ref[0:TQ + HALO, sl]], axis=0)
            if a == n_units - 1:
                return jnp.concatenate([cur_ref[tm - TQ - HALO:tm, sl], next_ref[:, sl]], axis=0)
            return cur_ref[a * TQ - HALO:(a + 1) * TQ + HALO, sl]

        def finish(o, lse):
            l1 = l1_ref[qrows, sl]
            l2 = l2_ref[qrows, sl]
            top = jnp.maximum(lse, jnp.maximum(l1, l2))
            w0 = jnp.exp2(lse - top)
            w1 = jnp.exp2(l1 - top)
            w2 = jnp.exp2(l2 - top)
            num = w0 * o + w1 * o1_ref[qrows, sl].astype(F32) + w2 * o2_ref[qrows, sl].astype(F32)
            ob = num / (w0 + w1 + w2)
            out_ref[qrows, sl] = (ob * g_ref[qrows, sl].astype(F32)).astype(BF16)

        return (lambda: (q_ref[qrows, sl], window(kp_ref, kc_ref, kn_ref)),
                lambda: bias_scr[slot],
                lambda: window(vp_ref, vc_ref, vn_ref),
                finish)

    units = [unit(a, hp) for a in range(n_units) for hp in range(PAIRS)]
    _attend_pipelined(units, s_scr, p_scr, ml_scr, lo, need_lse=True)


def _band_merge(proj3, o1, l1, o2, l2, tm=1024):
    batch, seq_len, _ = proj3.shape
    tiles = seq_len // tm
    hb = tm // HALO
    last_halo = seq_len // HALO - 1
    cq, ck, cv, cg = 4, 5, 6, 13

    def cur(col):
        return pl.BlockSpec((None, tm, COL), lambda b, i: (b, i, col))

    def prev(col):
        return pl.BlockSpec((None, HALO, COL), lambda b, i: (b, jnp.maximum(i * hb - 1, 0), col))

    def nxt(col):
        return pl.BlockSpec((None, HALO, COL), lambda b, i: (b, jnp.minimum((i + 1) * hb, last_halo), col))

    tok_spec = pl.BlockSpec((None, tm, COL), lambda b, i: (b, i, 0))
    return pl.pallas_call(
        functools.partial(_band_merge_kernel, tm=tm, tiles=tiles),
        grid=(batch, tiles),
        in_specs=[cur(cq), prev(ck), cur(ck), nxt(ck), prev(cv), cur(cv), nxt(cv)] + [tok_spec] * 4 + [cur(cg)],
        out_specs=tok_spec,
        out_shape=jax.ShapeDtypeStruct((batch, seq_len, COL), BF16),
        scratch_shapes=[pltpu.VMEM((3, 2 * TQ, TQ + 2 * HALO), F32)] + _attend_scratch(2 * TQ, TQ + 2 * HALO),
        compiler_params=pltpu.CompilerParams(
            dimension_semantics=("parallel", "arbitrary"), vmem_limit_bytes=VMEM_LIMIT),
        name="band_merge",
    )(*([proj3] * 7), o1, l1, o2, l2, proj3)


def _out_kernel(x_ref, ua_ref, ub_ref, sa_ref, sb_ref, wa_ref, wb_ref, wo_ref, y_ref):
    br_a = jnp.dot(ua_ref[...], wa_ref[...], preferred_element_type=F32)
    br_b = jnp.dot(ub_ref[...], wb_ref[...], preferred_element_type=F32)
    merged = sa_ref[...].astype(F32) * br_a + sb_ref[...].astype(F32) * br_b
    y_ref[...] = x_ref[...] + jnp.dot(merged.astype(BF16), wo_ref[...], preferred_element_type=F32)


def _output(x2d, ua, ub, proj, wa, wb, wo, tm=512):
    n_tok = x2d.shape[0]
    sig_a = (14 * COL) // D_MODEL
    sig_b = (16 * COL) // D_MODEL
    return pl.pallas_call(
        _out_kernel,
        grid=(n_tok // tm,),
        in_specs=[
            pl.BlockSpec((tm, D_MODEL), lambda i: (i, 0)),
            pl.BlockSpec((tm, COL), lambda i: (i, 0)),
            pl.BlockSpec((tm, COL), lambda i: (i, 0)),
            pl.BlockSpec((tm, D_MODEL), lambda i: (i, sig_a)),
            pl.BlockSpec((tm, D_MODEL), lambda i: (i, sig_b)),
            pl.BlockSpec((COL, D_MODEL), lambda i: (0, 0)),
            pl.BlockSpec((COL, D_MODEL), lambda i: (0, 0)),
            pl.BlockSpec((D_MODEL, D_MODEL), lambda i: (0, 0)),
        ],
        out_specs=pl.BlockSpec((tm, D_MODEL), lambda i: (i, 0)),
        out_shape=jax.ShapeDtypeStruct((n_tok, D_MODEL), F32),
        compiler_params=pltpu.CompilerParams(
            dimension_semantics=("parallel",), vmem_limit_bytes=VMEM_LIMIT),
        name="out_proj",
    )(x2d, ua, ub, proj, proj, wa, wb, wo)


def _rope_tables(seq_len):
    inv = ROPE_THETA ** (-jnp.arange(0, HEAD_DIM, 2, dtype=F32) / HEAD_DIM)
    ang = jnp.arange(seq_len, dtype=F32)[:, None] * inv[None, :]
    cos = jnp.cos(ang)
    sin = jnp.sin(ang)
    reps = LANES // HEAD_DIM
    cos_t = jnp.tile(jnp.concatenate([cos, cos], axis=1), (1, reps))
    sin_t = jnp.tile(jnp.concatenate([-sin, sin], axis=1), (1, reps))
    return cos_t, sin_t


def _tile_gains(qn_a, kn_a, qn_b, kn_b):
    scale = LOG2_E / math.sqrt(HEAD_DIM)
    heads = COL // HEAD_DIM
    rows = []
    for t in range(N_COL_TILES):
        if t == 0:
            g = qn_a
        elif t == 1:
            g = kn_a
        elif t in (4, 7, 10):
            g = qn_b
        elif t in (5, 8, 11):
            g = kn_b
        else:
            g = jnp.ones((HEAD_DIM,), F32)
        g = g.astype(F32)
        if t in _Q_TILES:
            g = g * scale
        rows.append(jnp.tile(g, heads))
    return jnp.stack(rows)[:, None, :]


def _layer(x, ng, w_bf, gains, bias_tab, bd, wa, wb, wo):
    batch, seq_len, _ = x.shape
    assert seq_len % TILE == 0
    x2d = x.reshape(batch * seq_len, D_MODEL)
    cos, sin = _rope_tables(seq_len)
    proj = _project(x2d, seq_len, ng, w_bf, gains, cos, sin, bd)
    proj3 = proj.reshape(batch, seq_len, D_IN)
    ua = _neighbourhood(proj3, bias_tab)
    o2, l2 = _band_dilated(proj3, 2, DIL_GROUPS[2][1])
    o1, l1 = _band_dilated(proj3, 1, DIL_GROUPS[1][1])
    ub = _band_merge(proj3, o1, l1, o2, l2)
    y = _output(x2d, ua.reshape(-1, COL), ub.reshape(-1, COL), proj, wa, wb, wo)
    return y.reshape(batch, seq_len, D_MODEL)


def kernel(x_prompt, x_sample, norm_gain, w_in, qn_a, kn_a, rel_bias_a, qn_b, kn_b,
           w_branch_a, w_branch_b, w_out):
    depth = norm_gain.shape[0]
    blk = jnp.arange(256) // HEAD_DIM
    bd = jnp.where(blk[:, None] == blk[None, :], 1.0 / HEAD_DIM, 0.0).astype(BF16)
    y_prompt, y_sample = x_prompt, x_sample
    for l in range(depth):
        ng = norm_gain[l].astype(F32)[None, :]
        w_bf = w_in[l].astype(BF16)
        gains = _tile_gains(qn_a[l], kn_a[l], qn_b[l], kn_b[l])
        bias_tab = _na_bias_table(rel_bias_a[l])
        wa = w_branch_a[l].astype(BF16)
        wb = w_branch_b[l].astype(BF16)
        wo = w_out[l].astype(BF16)
        y_prompt = _layer(y_prompt, ng, w_bf, gains, bias_tab, bd, wa, wb, wo)
        y_sample = _layer(y_sample, ng, w_bf, gains, bias_tab, bd, wa, wb, wo)
    return (y_prompt, y_sample)
```

```python
import functools
import math

import jax
import jax.numpy as jnp
from jax import lax
from jax.experimental import pallas as pl
from jax.experimental.pallas import tpu as pltpu

F32 = jnp.float32
BF16 = jnp.bfloat16

D_MODEL = 1024
HEAD_DIM = 64
GRID_W = 64
NA_HEADS = 8
NA_ROWS = 8
NA_COLS = 16
DIL_GROUPS = ((128, 1), (512, 4), (2048, 16))
ROPE_THETA = 10000.0
EPS = 1e-6
NEG = -1e30
LOG2_E = math.log2(math.e)

COL = 512
N_COL_TILES = 18
D_IN = COL * N_COL_TILES
LANES = 128
PAIRS = COL // LANES
HALO = 64
TILE = 2048
BLOCKS = TILE // HALO
TQ = 2 * HALO
NORM_CHUNK = 512
PLAIN_CHUNK = 256

_NORM_TILES = (0, 1)
_PLAIN_TILES = (2, 6)
_DILATED_TILES = {1: (4, 5, 6), 4: (7, 8, 9), 16: (10, 11, 12)}
_SILU_TILES = (3, 13)
_SIGM_TILES = (14, 15, 16, 17)
_Q_TILES = (0, 4, 7, 10)

VMEM_LIMIT = 56 * 1024 * 1024


def _any_of(j, members):
    return functools.reduce(jnp.logical_or, [j == m for m in members])


def _proj_kernel(x_ref, ng_ref, w_ref, gain_ref, cos_ref, sin_ref, bd_ref, o_ref, h_ref, perm_ref):
    j = pl.program_id(1)
    lane = lax.broadcasted_iota(jnp.int32, (1, LANES), 1)
    first_half = (lane % HEAD_DIM) < (HEAD_DIM // 2)

    def normalize_rows(r0, rc):
        x = x_ref[r0:r0 + rc, :]
        ms = jnp.mean(x * x, axis=-1, keepdims=True)
        h_ref[r0:r0 + rc, :] = (x * lax.rsqrt(ms + EPS) * ng_ref[...]).astype(BF16)

    def head_norm(a, r0, rc):
        sq = (a * a).astype(BF16)
        bd = bd_ref[...]
        ms = jnp.concatenate(
            [jnp.dot(sq[:, c:c + 256], bd, preferred_element_type=F32) for c in (0, 256)], axis=1)
        return a * lax.rsqrt(ms + EPS) * gain_ref[0]

    def rope(a, r0, rc):
        y = head_norm(a, r0, rc)
        cos = cos_ref[r0:r0 + rc, :]
        sin = sin_ref[r0:r0 + rc, :]
        parts = []
        for c in range(0, COL, LANES):
            yc = y[:, c:c + LANES]
            partner = jnp.where(first_half,
                                pltpu.roll(yc, LANES - HEAD_DIM // 2, 1),
                                pltpu.roll(yc, HEAD_DIM // 2, 1))
            parts.append(yc * cos + partner * sin)
        return jnp.concatenate(parts, axis=1)

    def sigmoid(a):
        return 0.5 * jnp.tanh(0.5 * a) + 0.5

    def run(members, epilogue, rc, dil=None, first=False):
        @pl.when(_any_of(j, members))
        def _():
            for r0 in range(0, TILE, rc):
                if first:
                    normalize_rows(r0, rc)
                a = jnp.dot(h_ref[r0:r0 + rc, :], w_ref[...], preferred_element_type=F32)
                y = epilogue(a, r0, rc)
                if dil is None:
                    o_ref[r0:r0 + rc, :] = y.astype(BF16)
                    continue
                for cp in range(PAIRS):
                    perm_ref[cp, r0:r0 + rc, :] = y[:, cp * LANES:(cp + 1) * LANES]
                span = HALO * dil
                done = r0 + rc
                for c in range(r0 // span, done // span):
                    for cp in range(PAIRS):
                        for ph in range(dil):
                            blk = c * dil + ph
                            rows = perm_ref[cp, pl.ds(c * span + ph, HALO, stride=dil), :]
                            o_ref[blk * HALO:(blk + 1) * HALO, cp * LANES:(cp + 1) * LANES] = rows.astype(BF16)

    plain = lambda a, r0, rc: a
    assert _NORM_TILES[0] == 0
    run(_NORM_TILES[:1], head_norm, NORM_CHUNK, first=True)
    run(_NORM_TILES[1:], head_norm, NORM_CHUNK)
    run(_PLAIN_TILES, plain, PLAIN_CHUNK)
    run(_SILU_TILES, lambda a, r0, rc: a * sigmoid(a), PLAIN_CHUNK)
    run(_SIGM_TILES, lambda a, r0, rc: sigmoid(a), PLAIN_CHUNK)
    for dil, (q_tile, k_tile, v_tile) in _DILATED_TILES.items():
        run((q_tile, k_tile), rope, NORM_CHUNK, None if dil == 1 else dil)
        if dil != 1:
            run((v_tile,), plain, PLAIN_CHUNK, dil)


def _project(x2d, seq_len, ng, w_bf, gains, cos, sin, bd):
    n_tok = x2d.shape[0]
    pos_blocks = seq_len // TILE
    return pl.pallas_call(
        _proj_kernel,
        grid=(n_tok // TILE, N_COL_TILES),
        in_specs=[
            pl.BlockSpec((TILE, D_MODEL), lambda i, j: (i, 0)),
            pl.BlockSpec((1, D_MODEL), lambda i, j: (0, 0)),
            pl.BlockSpec((D_MODEL, COL), lambda i, j: (0, j)),
            pl.BlockSpec((1, 1, COL), lambda i, j: (j, 0, 0)),
            pl.BlockSpec((TILE, LANES), lambda i, j: (i % pos_blocks, 0)),
            pl.BlockSpec((TILE, LANES), lambda i, j: (i % pos_blocks, 0)),
            pl.BlockSpec((256, 256), lambda i, j: (0, 0)),
        ],
        out_specs=pl.BlockSpec((TILE, COL), lambda i, j: (i, j)),
        out_shape=jax.ShapeDtypeStruct((n_tok, D_IN), BF16),
        scratch_shapes=[pltpu.VMEM((TILE, D_MODEL), BF16), pltpu.VMEM((PAIRS, TILE, LANES), F32)],
        compiler_params=pltpu.CompilerParams(
            dimension_semantics=("parallel", "arbitrary"), vmem_limit_bytes=VMEM_LIMIT),
        name="in_proj",
    )(x2d, ng, w_bf, gains, cos, sin, bd)


def _lane_lo():
    return lax.broadcasted_iota(jnp.int32, (1, LANES), 1) < HEAD_DIM


def _attend_pipelined(units, lo, need_lse):
    n = len(units)
    live = {}

    def scores(t):
        q, k = units[t][0]()
        zero = jnp.zeros_like(q)
        qq = jnp.concatenate([jnp.where(lo, q, zero), jnp.where(lo, zero, q)], axis=0)
        s = lax.dot_general(qq, k, (((1,), (1,)), ((), ())), preferred_element_type=F32) + units[t][1]()
        live[t] = (s, jnp.max(s, axis=-1, keepdims=True))

    def numerator(t):
        s, m = live.pop(t)
        p = jnp.exp2(s - m)
        l = jnp.sum(p, axis=-1, keepdims=True)
        live[t] = (p.astype(BF16), l, m + jnp.log2(l) if need_lse else None)

    def values(t):
        p, l, lse2 = live.pop(t)
        v = units[t][2]()
        o2 = jnp.dot(p, v, preferred_element_type=F32) / l
        nq = o2.shape[0] // 2
        lse = None
        if need_lse:
            lse = jnp.where(lo, lse2[:nq], lse2[nq:])
        units[t][3](jnp.where(lo, o2[:nq], o2[nq:]), lse)

    for t in range(n + 2):
        if t < n:
            scores(t)
        if 1 <= t <= n:
            numerator(t - 1)
        if t >= 2:
            values(t - 2)


def _na_kernel(q_ref, k_ref, v_ref, g_ref, bias_ref, o_ref, *, rows, rq):
    rb = pl.program_id(2)
    lo = _lane_lo()
    kwin = NA_ROWS * GRID_W

    def unit(rr):
        r = rb * rq + rr
        rs = jnp.clip(r - NA_ROWS // 2, 0, rows - NA_ROWS)
        shift = rs - r + (NA_ROWS - 1)
        window = pl.ds(pl.multiple_of(rs * GRID_W, GRID_W), kwin)
        rows_q = slice(rr * GRID_W, (rr + 1) * GRID_W)

        def load_bias():
            return jnp.concatenate([bias_ref[shift + 2 * i] for i in range(NA_ROWS // 2)], axis=1)

        def finish(o, lse):
            o_ref[rows_q, :] = (o * g_ref[rows_q, :].astype(F32)).astype(BF16)

        return (lambda: (q_ref[rows_q, :], k_ref[window, :]), load_bias, lambda: v_ref[window, :], finish)

    _attend_pipelined([unit(rr) for rr in range(rq)], lo, need_lse=False)


def _neighbourhood(proj3, bias_tab, rq=32):
    batch, seq_len, _ = proj3.shape
    rows = seq_len // GRID_W
    tq = rq * GRID_W
    return pl.pallas_call(
        functools.partial(_na_kernel, rows=rows, rq=rq),
        grid=(batch, PAIRS, rows // rq),
        in_specs=[
            pl.BlockSpec((None, tq, LANES), lambda b, hp, rb: (b, rb, 0 * PAIRS + hp)),
            pl.BlockSpec((None, seq_len, LANES), lambda b, hp, rb: (b, 0, 1 * PAIRS + hp)),
            pl.BlockSpec((None, seq_len, LANES), lambda b, hp, rb: (b, 0, 2 * PAIRS + hp)),
            pl.BlockSpec((None, tq, LANES), lambda b, hp, rb: (b, rb, 3 * PAIRS + hp)),
            pl.BlockSpec((None, 2 * NA_ROWS - 2, 2 * GRID_W, 2 * GRID_W), lambda b, hp, rb: (hp, 0, 0, 0)),
        ],
        out_specs=pl.BlockSpec((None, tq, LANES), lambda b, hp, rb: (b, rb, hp)),
        out_shape=jax.ShapeDtypeStruct((batch, seq_len, COL), BF16),
        compiler_params=pltpu.CompilerParams(
            dimension_semantics=("parallel", "parallel", "arbitrary"), vmem_limit_bytes=VMEM_LIMIT),
        name="na_attn",
    )(proj3, proj3, proj3, proj3, bias_tab)


def _na_bias_table(rel_bias):
    heads, n_dr, n_dc = rel_bias.shape
    period = 2 * GRID_W
    diag = jnp.pad(rel_bias.astype(F32), ((0, 0), (0, 0), (0, period - n_dc)))
    diag = jnp.roll(diag, -(NA_COLS - 1), axis=-1)
    tiled = jnp.tile(diag, (1, 1, GRID_W))[:, :, :GRID_W * (period - 1)]
    toeplitz = tiled.reshape(heads, n_dr, GRID_W, period - 1)[..., :GRID_W]
    c = jnp.arange(GRID_W)
    cs = jnp.clip(c - NA_COLS // 2, 0, GRID_W - NA_COLS)
    ok = (c[None, :] >= cs[:, None]) & (c[None, :] < cs[:, None] + NA_COLS)
    toeplitz = jnp.where(ok[None, None], toeplitz * LOG2_E, NEG)
    pairs = jnp.concatenate([toeplitz[:, :n_dr - 1], toeplitz[:, 1:]], axis=-1)
    pairs = pairs.reshape(heads // 2, 2, n_dr - 1, GRID_W, 2 * GRID_W).transpose(0, 2, 1, 3, 4)
    return pairs.reshape(heads // 2, n_dr - 1, 2 * GRID_W, 2 * GRID_W)


def _band_bias_table():
    nk = TQ + 2 * HALO
    qpos = jnp.arange(2 * TQ)[:, None] % TQ
    col = jnp.arange(nk)[None, :]
    band = jnp.abs(col - HALO - qpos) <= HALO
    masks = [band & ((col >= HALO) | (not first)) & ((col < TQ + HALO) | (not last))
             for last in (False, True) for first in (False, True)]
    return jnp.where(jnp.stack(masks), 0.0, NEG).astype(F32)


def _band_bias_index(at_start, at_end):
    idx = 0
    if at_start is not None:
        idx = idx + at_start.astype(jnp.int32)
    if at_end is not None:
        idx = idx + 2 * at_end.astype(jnp.int32)
    return idx


def _band_dilated_kernel(q_ref, kp_ref, kc_ref, kn_ref, vp_ref, vc_ref, vn_ref, bias_ref, out_ref, lse_ref,
                         o_scr, *, dil, tiles):
    i = pl.program_id(1)
    lo = _lane_lo()
    ch = TQ // HALO
    n_units = BLOCKS // dil // ch
    first, last = i == 0, i == tiles - 1

    def rows(chunk, ph):
        return slice((chunk * dil + ph) * HALO, (chunk * dil + ph + 1) * HALO)

    def unit(ph, a, hp):
        sl = slice(hp * LANES, (hp + 1) * LANES)
        halo = slice(ph * HALO, (ph + 1) * HALO)
        bias_idx = _band_bias_index(first if a == 0 else None, last if a == n_units - 1 else None)

        def chunks(ref):
            return [ref[rows(c, ph), sl] for c in range(a * ch, (a + 1) * ch)]

        def window(prev_ref, cur_ref, next_ref):
            head = prev_ref[halo, sl] if a == 0 else cur_ref[rows(a * ch - 1, ph), sl]
            tail = next_ref[halo, sl] if a == n_units - 1 else cur_ref[rows((a + 1) * ch, ph), sl]
            return jnp.concatenate([head] + chunks(cur_ref) + [tail], axis=0)

        def load_qk():
            return jnp.concatenate(chunks(q_ref), axis=0), window(kp_ref, kc_ref, kn_ref)

        def finish(o, lse):
            tok = pl.ds(a * TQ * dil + ph, TQ, stride=dil)
            o_scr[hp, tok, :] = o
            lse_ref[hp, tok, :] = lse

        return load_qk, lambda: bias_ref[bias_idx], lambda: window(vp_ref, vc_ref, vn_ref), finish

    units = [unit(ph, a, hp) for ph in range(dil) for a in range(n_units) for hp in range(PAIRS)]
    _attend_pipelined(units, lo, need_lse=True)
    for hp in range(PAIRS):
        out_ref[:, hp * LANES:(hp + 1) * LANES] = o_scr[hp].astype(BF16)


def _band_dilated(proj3, bias_tab, group, dil):
    batch, seq_len, _ = proj3.shape
    tiles = seq_len // TILE
    cq, ck, cv = 4 + 3 * group, 5 + 3 * group, 6 + 3 * group
    halo_rows = dil * HALO
    per_tile = TILE // halo_rows
    last_unit = seq_len // halo_rows - 1

    def cur(col):
        return pl.BlockSpec((None, TILE, COL), lambda b, i: (b, i, col))

    def prev(col):
        return pl.BlockSpec((None, halo_rows, COL), lambda b, i: (b, jnp.maximum(i * per_tile - 1, 0), col))

    def nxt(col):
        return pl.BlockSpec((None, halo_rows, COL),
                            lambda b, i: (b, jnp.minimum((i + 1) * per_tile, last_unit), col))

    return pl.pallas_call(
        functools.partial(_band_dilated_kernel, dil=dil, tiles=tiles),
        grid=(batch, tiles),
        in_specs=[cur(cq), prev(ck), cur(ck), nxt(ck), prev(cv), cur(cv), nxt(cv),
                  pl.BlockSpec(bias_tab.shape, lambda b, i: (0, 0, 0))],
        out_specs=[pl.BlockSpec((None, TILE, COL), lambda b, i: (b, i, 0)),
                   pl.BlockSpec((None, PAIRS, TILE, LANES), lambda b, i: (b, 0, i, 0))],
        out_shape=[jax.ShapeDtypeStruct((batch, seq_len, COL), BF16),
                   jax.ShapeDtypeStruct((batch, PAIRS, seq_len, LANES), F32)],
        scratch_shapes=[pltpu.VMEM((PAIRS, TILE, LANES), F32)],
        compiler_params=pltpu.CompilerParams(
            dimension_semantics=("parallel", "arbitrary"), vmem_limit_bytes=VMEM_LIMIT),
        name="band_d%d" % dil,
    )(*([proj3] * 7), bias_tab)


def _band_merge_kernel(q_ref, kp_ref, kc_ref, kn_ref, vp_ref, vc_ref, vn_ref, bias_ref,
                       o1_ref, l1_ref, o2_ref, l2_ref, g_ref, out_ref, *, tm, tiles):
    i = pl.program_id(1)
    lo = _lane_lo()
    n_units = tm // TQ
    first, last = i == 0, i == tiles - 1

    def unit(a, hp):
        sl = slice(hp * LANES, (hp + 1) * LANES)
        qrows = slice(a * TQ, (a + 1) * TQ)
        bias_idx = _band_bias_index(first if a == 0 else None, last if a == n_units - 1 else None)

        def window(prev_ref, cur_ref, next_ref):
            if a == 0:
                return jnp.concatenate([prev_ref[:, sl], cur_ref[0:TQ + HALO, sl]], axis=0)
            if a == n_units - 1:
                return jnp.concatenate([cur_ref[tm - TQ - HALO:tm, sl], next_ref[:, sl]], axis=0)
            return cur_ref[a * TQ - HALO:(a + 1) * TQ + HALO, sl]

        def finish(o, lse):
            l1 = l1_ref[hp, qrows, :]
            l2 = l2_ref[hp, qrows, :]
            top = jnp.maximum(lse, jnp.maximum(l1, l2))
            w0 = jnp.exp2(lse - top)
            w1 = jnp.exp2(l1 - top)
            w2 = jnp.exp2(l2 - top)
            num = w0 * o + w1 * o1_ref[qrows, sl].astype(F32) + w2 * o2_ref[qrows, sl].astype(F32)
            ob = num / (w0 + w1 + w2)
            out_ref[qrows, sl] = (ob * g_ref[qrows, sl].astype(F32)).astype(BF16)

        return (lambda: (q_ref[qrows, sl], window(kp_ref, kc_ref, kn_ref)),
                lambda: bias_ref[bias_idx],
                lambda: window(vp_ref, vc_ref, vn_ref),
                finish)

    units = [unit(a, hp) for a in range(n_units) for hp in range(PAIRS)]
    _attend_pipelined(units, lo, need_lse=True)


def _band_merge(proj3, bias_tab, o1, l1, o2, l2, tm=1024):
    batch, seq_len, _ = proj3.shape
    tiles = seq_len // tm
    hb = tm // HALO
    last_halo = seq_len // HALO - 1
    cq, ck, cv, cg = 4, 5, 6, 13

    def cur(col):
        return pl.BlockSpec((None, tm, COL), lambda b, i: (b, i, col))

    def prev(col):
        return pl.BlockSpec((None, HALO, COL), lambda b, i: (b, jnp.maximum(i * hb - 1, 0), col))

    def nxt(col):
        return pl.BlockSpec((None, HALO, COL), lambda b, i: (b, jnp.minimum((i + 1) * hb, last_halo), col))

    tok_spec = pl.BlockSpec((None, tm, COL), lambda b, i: (b, i, 0))
    lse_spec = pl.BlockSpec((None, PAIRS, tm, LANES), lambda b, i: (b, 0, i, 0))
    return pl.pallas_call(
        functools.partial(_band_merge_kernel, tm=tm, tiles=tiles),
        grid=(batch, tiles),
        in_specs=[cur(cq), prev(ck), cur(ck), nxt(ck), prev(cv), cur(cv), nxt(cv),
                  pl.BlockSpec(bias_tab.shape, lambda b, i: (0, 0, 0)),
                  tok_spec, lse_spec, tok_spec, lse_spec, cur(cg)],
        out_specs=tok_spec,
        out_shape=jax.ShapeDtypeStruct((batch, seq_len, COL), BF16),
        compiler_params=pltpu.CompilerParams(
            dimension_semantics=("parallel", "arbitrary"), vmem_limit_bytes=VMEM_LIMIT),
        name="band_merge",
    )(*([proj3] * 7), bias_tab, o1, l1, o2, l2, proj3)


def _out_kernel(x_ref, ua_ref, ub_ref, sa_ref, sb_ref, wa_ref, wb_ref, wo_ref, y_ref):
    br_a = jnp.dot(ua_ref[...], wa_ref[...], preferred_element_type=F32)
    br_b = jnp.dot(ub_ref[...], wb_ref[...], preferred_element_type=F32)
    merged = sa_ref[...].astype(F32) * br_a + sb_ref[...].astype(F32) * br_b
    y_ref[...] = x_ref[...] + jnp.dot(merged.astype(BF16), wo_ref[...], preferred_element_type=F32)


def _output(x2d, ua, ub, proj, wa, wb, wo, tm=512):
    n_tok = x2d.shape[0]
    sig_a = (14 * COL) // D_MODEL
    sig_b = (16 * COL) // D_MODEL
    return pl.pallas_call(
        _out_kernel,
        grid=(n_tok // tm,),
        in_specs=[
            pl.BlockSpec((tm, D_MODEL), lambda i: (i, 0)),
            pl.BlockSpec((tm, COL), lambda i: (i, 0)),
            pl.BlockSpec((tm, COL), lambda i: (i, 0)),
            pl.BlockSpec((tm, D_MODEL), lambda i: (i, sig_a)),
            pl.BlockSpec((tm, D_MODEL), lambda i: (i, sig_b)),
            pl.BlockSpec((COL, D_MODEL), lambda i: (0, 0)),
            pl.BlockSpec((COL, D_MODEL), lambda i: (0, 0)),
            pl.BlockSpec((D_MODEL, D_MODEL), lambda i: (0, 0)),
        ],
        out_specs=pl.BlockSpec((tm, D_MODEL), lambda i: (i, 0)),
        out_shape=jax.ShapeDtypeStruct((n_tok, D_MODEL), F32),
        compiler_params=pltpu.CompilerParams(
            dimension_semantics=("parallel",), vmem_limit_bytes=VMEM_LIMIT),
        name="out_proj",
    )(x2d, ua, ub, proj, proj, wa, wb, wo)


def _rope_tables(seq_len):
    inv = ROPE_THETA ** (-jnp.arange(0, HEAD_DIM, 2, dtype=F32) / HEAD_DIM)
    ang = jnp.arange(seq_len, dtype=F32)[:, None] * inv[None, :]
    cos = jnp.cos(ang)
    sin = jnp.sin(ang)
    reps = LANES // HEAD_DIM
    cos_t = jnp.tile(jnp.concatenate([cos, cos], axis=1), (1, reps))
    sin_t = jnp.tile(jnp.concatenate([-sin, sin], axis=1), (1, reps))
    return cos_t, sin_t


def _tile_gains(qn_a, kn_a, qn_b, kn_b):
    scale = LOG2_E / math.sqrt(HEAD_DIM)
    heads = COL // HEAD_DIM
    rows = []
    for t in range(N_COL_TILES):
        if t == 0:
            g = qn_a
        elif t == 1:
            g = kn_a
        elif t in (4, 7, 10):
            g = qn_b
        elif t in (5, 8, 11):
            g = kn_b
        else:
            g = jnp.ones((HEAD_DIM,), F32)
        g = g.astype(F32)
        if t in _Q_TILES:
            g = g * scale
        rows.append(jnp.tile(g, heads))
    return jnp.stack(rows)[:, None, :]


def _layer(x, ng, w_bf, gains, bias_tab, bd, wa, wb, wo):
    batch, seq_len, _ = x.shape
    assert seq_len % TILE == 0
    x2d = x.reshape(batch * seq_len, D_MODEL)
    cos, sin = _rope_tables(seq_len)
    proj = _project(x2d, seq_len, ng, w_bf, gains, cos, sin, bd)
    proj3 = proj.reshape(batch, seq_len, D_IN)
    ua = _neighbourhood(proj3, bias_tab)
    band_tab = _band_bias_table()
    o2, l2 = _band_dilated(proj3, band_tab, 2, DIL_GROUPS[2][1])
    o1, l1 = _band_dilated(proj3, band_tab, 1, DIL_GROUPS[1][1])
    ub = _band_merge(proj3, band_tab, o1, l1, o2, l2)
    y = _output(x2d, ua.reshape(-1, COL), ub.reshape(-1, COL), proj, wa, wb, wo)
    return y.reshape(batch, seq_len, D_MODEL)


def kernel(x_prompt, x_sample, norm_gain, w_in, qn_a, kn_a, rel_bias_a, qn_b, kn_b,
           w_branch_a, w_branch_b, w_out):
    depth = norm_gain.shape[0]
    blk = jnp.arange(256) // HEAD_DIM
    bd = jnp.where(blk[:, None] == blk[None, :], 1.0 / HEAD_DIM, 0.0).astype(BF16)
    y_prompt, y_sample = x_prompt, x_sample
    for l in range(depth):
        ng = norm_gain[l].astype(F32)[None, :]
        w_bf = w_in[l].astype(BF16)
        gains = _tile_gains(qn_a[l], kn_a[l], qn_b[l], kn_b[l])
        bias_tab = _na_bias_table(rel_bias_a[l])
        wa = w_branch_a[l].astype(BF16)
        wb = w_branch_b[l].astype(BF16)
        wo = w_out[l].astype(BF16)
        y_prompt = _layer(y_prompt, ng, w_bf, gains, bias_tab, bd, wa, wb, wo)
        y_sample = _layer(y_sample, ng, w_bf, gains, bias_tab, bd, wa, wb, wo)
    return (y_prompt, y_sample)
```

```python
import functools
import math

import jax
import jax.numpy as jnp
from jax import lax
from jax.experimental import pallas as pl
from jax.experimental.pallas import tpu as pltpu

F32 = jnp.float32
BF16 = jnp.bfloat16

D_MODEL = 1024
HEAD_DIM = 64
GRID_W = 64
NA_HEADS = 8
NA_ROWS = 8
NA_COLS = 16
DIL_GROUPS = ((128, 1), (512, 4), (2048, 16))
ROPE_THETA = 10000.0
EPS = 1e-6
NEG = -1e30
LOG2_E = math.log2(math.e)

COL = 512
N_COL_TILES = 18
D_IN = COL * N_COL_TILES
LANES = 128
PAIRS = COL // LANES
HALO = 64
TILE = 2048
BLOCKS = TILE // HALO
TQ = 2 * HALO
NORM_CHUNK = 512
PLAIN_CHUNK = 256

_NORM_TILES = (0, 1)
_PLAIN_TILES = (2, 6)
_DILATED_TILES = {1: (4, 5, 6), 4: (7, 8, 9), 16: (10, 11, 12)}
_SILU_TILES = (3, 13)
_SIGM_TILES = (14, 15, 16, 17)
_Q_TILES = (0, 4, 7, 10)

VMEM_LIMIT = 56 * 1024 * 1024


def _any_of(j, members):
    return functools.reduce(jnp.logical_or, [j == m for m in members])


def _proj_kernel(x_ref, ng_ref, w_ref, gain_ref, cos_ref, sin_ref, bd_ref, o_ref, h_ref, perm_ref):
    j = pl.program_id(1)
    lane = lax.broadcasted_iota(jnp.int32, (1, LANES), 1)
    first_half = (lane % HEAD_DIM) < (HEAD_DIM // 2)

    def normalize_rows(r0, rc):
        x = x_ref[r0:r0 + rc, :]
        ms = jnp.mean(x * x, axis=-1, keepdims=True)
        h_ref[r0:r0 + rc, :] = (x * lax.rsqrt(ms + EPS) * ng_ref[...]).astype(BF16)

    def head_norm(a, r0, rc):
        sq = (a * a).astype(BF16)
        bd = bd_ref[...]
        ms = jnp.concatenate(
            [jnp.dot(sq[:, c:c + 256], bd, preferred_element_type=F32) for c in (0, 256)], axis=1)
        return a * lax.rsqrt(ms + EPS) * gain_ref[0]

    def rope(a, r0, rc):
        y = head_norm(a, r0, rc)
        cos = cos_ref[r0:r0 + rc, :]
        sin = sin_ref[r0:r0 + rc, :]
        parts = []
        for c in range(0, COL, LANES):
            yc = y[:, c:c + LANES]
            partner = jnp.where(first_half,
                                pltpu.roll(yc, LANES - HEAD_DIM // 2, 1),
                                pltpu.roll(yc, HEAD_DIM // 2, 1))
            parts.append(yc * cos + partner * sin)
        return jnp.concatenate(parts, axis=1)

    def sigmoid(a):
        return 0.5 * jnp.tanh(0.5 * a) + 0.5

    def run(members, epilogue, rc, dil=None, first=False):
        @pl.when(_any_of(j, members))
        def _():
            for r0 in range(0, TILE, rc):
                if first:
                    normalize_rows(r0, rc)
                a = jnp.dot(h_ref[r0:r0 + rc, :], w_ref[...], preferred_element_type=F32)
                y = epilogue(a, r0, rc)
                if dil is None:
                    o_ref[r0:r0 + rc, :] = y.astype(BF16)
                    continue
                for cp in range(PAIRS):
                    perm_ref[cp, r0:r0 + rc, :] = y[:, cp * LANES:(cp + 1) * LANES]
                span = HALO * dil
                done = r0 + rc
                for c in range(r0 // span, done // span):
                    for cp in range(PAIRS):
                        for ph in range(dil):
                            blk = c * dil + ph
                            rows = perm_ref[cp, pl.ds(c * span + ph, HALO, stride=dil), :]
                            o_ref[blk * HALO:(blk + 1) * HALO, cp * LANES:(cp + 1) * LANES] = rows.astype(BF16)

    plain = lambda a, r0, rc: a
    assert _NORM_TILES[0] == 0
    run(_NORM_TILES[:1], head_norm, NORM_CHUNK, first=True)
    run(_NORM_TILES[1:], head_norm, NORM_CHUNK)
    run(_PLAIN_TILES, plain, PLAIN_CHUNK)
    run(_SILU_TILES, lambda a, r0, rc: a * sigmoid(a), PLAIN_CHUNK)
    run(_SIGM_TILES, lambda a, r0, rc: sigmoid(a), PLAIN_CHUNK)
    for dil, (q_tile, k_tile, v_tile) in _DILATED_TILES.items():
        run((q_tile, k_tile), rope, NORM_CHUNK, None if dil == 1 else dil)
        if dil != 1:
            run((v_tile,), plain, PLAIN_CHUNK, dil)


def _project(x2d, seq_len, ng, w_bf, gains, cos, sin, bd):
    n_tok = x2d.shape[0]
    pos_blocks = seq_len // TILE
    return pl.pallas_call(
        _proj_kernel,
        grid=(n_tok // TILE, N_COL_TILES),
        in_specs=[
            pl.BlockSpec((TILE, D_MODEL), lambda i, j: (i, 0)),
            pl.BlockSpec((1, D_MODEL), lambda i, j: (0, 0)),
            pl.BlockSpec((D_MODEL, COL), lambda i, j: (0, j)),
            pl.BlockSpec((1, 1, COL), lambda i, j: (j, 0, 0)),
            pl.BlockSpec((TILE, LANES), lambda i, j: (i % pos_blocks, 0)),
            pl.BlockSpec((TILE, LANES), lambda i, j: (i % pos_blocks, 0)),
            pl.BlockSpec((256, 256), lambda i, j: (0, 0)),
        ],
        out_specs=pl.BlockSpec((TILE, COL), lambda i, j: (i, j)),
        out_shape=jax.ShapeDtypeStruct((n_tok, D_IN), BF16),
        scratch_shapes=[pltpu.VMEM((TILE, D_MODEL), BF16), pltpu.VMEM((PAIRS, TILE, LANES), F32)],
        compiler_params=pltpu.CompilerParams(
            dimension_semantics=("parallel", "arbitrary"), vmem_limit_bytes=VMEM_LIMIT),
        name="in_proj",
    )(x2d, ng, w_bf, gains, cos, sin, bd)


def _lane_lo():
    return lax.broadcasted_iota(jnp.int32, (1, LANES), 1) < HEAD_DIM


def _attend_pipelined(units, lo, need_lse):
    n = len(units)
    live = {}

    def scores(t):
        q, k = units[t][0]()
        zero = jnp.zeros_like(q)
        qq = jnp.concatenate([jnp.where(lo, q, zero), jnp.where(lo, zero, q)], axis=0)
        s = lax.dot_general(qq, k, (((1,), (1,)), ((), ())), preferred_element_type=F32) + units[t][1]()
        live[t] = (s, jnp.max(s, axis=-1, keepdims=True))

    def numerator(t):
        s, m = live.pop(t)
        p = jnp.exp2(s - m)
        l = jnp.sum(p, axis=-1, keepdims=True)
        live[t] = (p.astype(BF16), l, m + jnp.log2(l) if need_lse else None)

    def values(t):
        p, l, lse2 = live.pop(t)
        v = units[t][2]()
        o2 = jnp.dot(p, v, preferred_element_type=F32) / l
        nq = o2.shape[0] // 2
        lse = None
        if need_lse:
            lse = jnp.where(lo, lse2[:nq], lse2[nq:])
        units[t][3](jnp.where(lo, o2[:nq], o2[nq:]), lse)

    for t in range(n + 2):
        if t < n:
            scores(t)
        if 1 <= t <= n:
            numerator(t - 1)
        if t >= 2:
            values(t - 2)


def _na_kernel(q_ref, k_ref, v_ref, g_ref, bias_ref, o_ref, *, rows, rq):
    rb = pl.program_id(2)
    lo = _lane_lo()
    kwin = NA_ROWS * GRID_W

    def unit(rr):
        r = rb * rq + rr
        rs = jnp.clip(r - NA_ROWS // 2, 0, rows - NA_ROWS)
        shift = rs - r + (NA_ROWS - 1)
        window = pl.ds(pl.multiple_of(rs * GRID_W, GRID_W), kwin)
        rows_q = slice(rr * GRID_W, (rr + 1) * GRID_W)

        def load_bias():
            return jnp.concatenate([bias_ref[shift + 2 * i] for i in range(NA_ROWS // 2)], axis=1)

        def finish(o, lse):
            o_ref[rows_q, :] = (o * g_ref[rows_q, :].astype(F32)).astype(BF16)

        return (lambda: (q_ref[rows_q, :], k_ref[window, :]), load_bias, lambda: v_ref[window, :], finish)

    _attend_pipelined([unit(rr) for rr in range(rq)], lo, need_lse=False)


def _neighbourhood(proj3, bias_tab, rq=64):
    batch, seq_len, _ = proj3.shape
    rows = seq_len // GRID_W
    tq = rq * GRID_W
    return pl.pallas_call(
        functools.partial(_na_kernel, rows=rows, rq=rq),
        grid=(batch, PAIRS, rows // rq),
        in_specs=[
            pl.BlockSpec((None, tq, LANES), lambda b, hp, rb: (b, rb, 0 * PAIRS + hp)),
            pl.BlockSpec((None, seq_len, LANES), lambda b, hp, rb: (b, 0, 1 * PAIRS + hp)),
            pl.BlockSpec((None, seq_len, LANES), lambda b, hp, rb: (b, 0, 2 * PAIRS + hp)),
            pl.BlockSpec((None, tq, LANES), lambda b, hp, rb: (b, rb, 3 * PAIRS + hp)),
            pl.BlockSpec((None, 2 * NA_ROWS - 2, 2 * GRID_W, 2 * GRID_W), lambda b, hp, rb: (hp, 0, 0, 0)),
        ],
        out_specs=pl.BlockSpec((None, tq, LANES), lambda b, hp, rb: (b, rb, hp)),
        out_shape=jax.ShapeDtypeStruct((batch, seq_len, COL), BF16),
        compiler_params=pltpu.CompilerParams(
            dimension_semantics=("parallel", "parallel", "arbitrary"), vmem_limit_bytes=VMEM_LIMIT),
        name="na_attn",
    )(proj3, proj3, proj3, proj3, bias_tab)


def _na_bias_table(rel_bias):
    heads, n_dr, n_dc = rel_bias.shape
    period = 2 * GRID_W
    diag = jnp.pad(rel_bias.astype(F32), ((0, 0), (0, 0), (0, period - n_dc)))
    diag = jnp.roll(diag, -(NA_COLS - 1), axis=-1)
    tiled = jnp.tile(diag, (1, 1, GRID_W))[:, :, :GRID_W * (period - 1)]
    toeplitz = tiled.reshape(heads, n_dr, GRID_W, period - 1)[..., :GRID_W]
    c = jnp.arange(GRID_W)
    cs = jnp.clip(c - NA_COLS // 2, 0, GRID_W - NA_COLS)
    ok = (c[None, :] >= cs[:, None]) & (c[None, :] < cs[:, None] + NA_COLS)
    toeplitz = jnp.where(ok[None, None], toeplitz * LOG2_E, NEG)
    pairs = jnp.concatenate([toeplitz[:, :n_dr - 1], toeplitz[:, 1:]], axis=-1)
    pairs = pairs.reshape(heads // 2, 2, n_dr - 1, GRID_W, 2 * GRID_W).transpose(0, 2, 1, 3, 4)
    return pairs.reshape(heads // 2, n_dr - 1, 2 * GRID_W, 2 * GRID_W)


def _band_bias_table():
    nk = TQ + 2 * HALO
    qpos = jnp.arange(2 * TQ)[:, None] % TQ
    col = jnp.arange(nk)[None, :]
    band = jnp.abs(col - HALO - qpos) <= HALO
    masks = [band & ((col >= HALO) | (not first)) & ((col < TQ + HALO) | (not last))
             for last in (False, True) for first in (False, True)]
    return jnp.where(jnp.stack(masks), 0.0, NEG).astype(F32)


def _band_bias_index(at_start, at_end):
    idx = 0
    if at_start is not None:
        idx = idx + at_start.astype(jnp.int32)
    if at_end is not None:
        idx = idx + 2 * at_end.astype(jnp.int32)
    return idx


def _band_dilated_kernel(q_ref, kp_ref, kc_ref, kn_ref, vp_ref, vc_ref, vn_ref, bias_ref, out_ref, lse_ref,
                         o_scr, *, dil, tiles):
    i = pl.program_id(1)
    lo = _lane_lo()
    ch = TQ // HALO
    n_units = BLOCKS // dil // ch
    first, last = i == 0, i == tiles - 1

    def rows(chunk, ph):
        return slice((chunk * dil + ph) * HALO, (chunk * dil + ph + 1) * HALO)

    def unit(ph, a, hp):
        sl = slice(hp * LANES, (hp + 1) * LANES)
        halo = slice(ph * HALO, (ph + 1) * HALO)
        bias_idx = _band_bias_index(first if a == 0 else None, last if a == n_units - 1 else None)

        def chunks(ref):
            return [ref[rows(c, ph), sl] for c in range(a * ch, (a + 1) * ch)]

        def window(prev_ref, cur_ref, next_ref):
            head = prev_ref[halo, sl] if a == 0 else cur_ref[rows(a * ch - 1, ph), sl]
            tail = next_ref[halo, sl] if a == n_units - 1 else cur_ref[rows((a + 1) * ch, ph), sl]
            return jnp.concatenate([head] + chunks(cur_ref) + [tail], axis=0)

        def load_qk():
            return jnp.concatenate(chunks(q_ref), axis=0), window(kp_ref, kc_ref, kn_ref)

        def finish(o, lse):
            tok = pl.ds(a * TQ * dil + ph, TQ, stride=dil)
            o_scr[hp, tok, :] = o
            lse_ref[hp, tok, :] = lse

        return load_qk, lambda: bias_ref[bias_idx], lambda: window(vp_ref, vc_ref, vn_ref), finish

    units = [unit(ph, a, hp) for ph in range(dil) for a in range(n_units) for hp in range(PAIRS)]
    _attend_pipelined(units, lo, need_lse=True)
    for hp in range(PAIRS):
        out_ref[:, hp * LANES:(hp + 1) * LANES] = o_scr[hp].astype(BF16)


def _band_dilated(proj3, bias_tab, group, dil):
    batch, seq_len, _ = proj3.shape
    tiles = seq_len // TILE
    cq, ck, cv = 4 + 3 * group, 5 + 3 * group, 6 + 3 * group
    halo_rows = dil * HALO
    per_tile = TILE // halo_rows
    last_unit = seq_len // halo_rows - 1

    def cur(col):
        return pl.BlockSpec((None, TILE, COL), lambda b, i: (b, i, col))

    def prev(col):
        return pl.BlockSpec((None, halo_rows, COL), lambda b, i: (b, jnp.maximum(i * per_tile - 1, 0), col))

    def nxt(col):
        return pl.BlockSpec((None, halo_rows, COL),
                            lambda b, i: (b, jnp.minimum((i + 1) * per_tile, last_unit), col))

    return pl.pallas_call(
        functools.partial(_band_dilated_kernel, dil=dil, tiles=tiles),
        grid=(batch, tiles),
        in_specs=[cur(cq), prev(ck), cur(ck), nxt(ck), prev(cv), cur(cv), nxt(cv),
                  pl.BlockSpec(bias_tab.shape, lambda b, i: (0, 0, 0))],
        out_specs=[pl.BlockSpec((None, TILE, COL), lambda b, i: (b, i, 0)),
                   pl.BlockSpec((None, PAIRS, TILE, LANES), lambda b, i: (b, 0, i, 0))],
        out_shape=[jax.ShapeDtypeStruct((batch, seq_len, COL), BF16),
                   jax.ShapeDtypeStruct((batch, PAIRS, seq_len, LANES), F32)],
        scratch_shapes=[pltpu.VMEM((PAIRS, TILE, LANES), F32)],
        compiler_params=pltpu.CompilerParams(
            dimension_semantics=("parallel", "arbitrary"), vmem_limit_bytes=VMEM_LIMIT),
        name="band_d%d" % dil,
    )(*([proj3] * 7), bias_tab)


def _band_merge_kernel(q_ref, kp_ref, kc_ref, kn_ref, vp_ref, vc_ref, vn_ref, bias_ref,
                       o1_ref, l1_ref, o2_ref, l2_ref, g_ref, out_ref, *, tm, tiles):
    i = pl.program_id(1)
    lo = _lane_lo()
    n_units = tm // TQ
    first, last = i == 0, i == tiles - 1

    def unit(a, hp):
        sl = slice(hp * LANES, (hp + 1) * LANES)
        qrows = slice(a * TQ, (a + 1) * TQ)
        bias_idx = _band_bias_index(first if a == 0 else None, last if a == n_units - 1 else None)

        def window(prev_ref, cur_ref, next_ref):
            if a == 0:
                return jnp.concatenate([prev_ref[:, sl], cur_ref[0:TQ + HALO, sl]], axis=0)
            if a == n_units - 1:
                return jnp.concatenate([cur_ref[tm - TQ - HALO:tm, sl], next_ref[:, sl]], axis=0)
            return cur_ref[a * TQ - HALO:(a + 1) * TQ + HALO, sl]

        def finish(o, lse):
            l1 = l1_ref[hp, qrows, :]
            l2 = l2_ref[hp, qrows, :]
            top = jnp.maximum(lse, jnp.maximum(l1, l2))
            w0 = jnp.exp2(lse - top)
            w1 = jnp.exp2(l1 - top)
            w2 = jnp.exp2(l2 - top)
            num = w0 * o + w1 * o1_ref[qrows, sl].astype(F32) + w2 * o2_ref[qrows, sl].astype(F32)
            ob = num / (w0 + w1 + w2)
            out_ref[qrows, sl] = (ob * g_ref[qrows, sl].astype(F32)).astype(BF16)

        return (lambda: (q_ref[qrows, sl], window(kp_ref, kc_ref, kn_ref)),
                lambda: bias_ref[bias_idx],
                lambda: window(vp_ref, vc_ref, vn_ref),
                finish)

    units = [unit(a, hp) for a in range(n_units) for hp in range(PAIRS)]
    _attend_pipelined(units, lo, need_lse=True)


def _band_merge(proj3, bias_tab, o1, l1, o2, l2, tm=1024):
    batch, seq_len, _ = proj3.shape
    tiles = seq_len // tm
    hb = tm // HALO
    last_halo = seq_len // HALO - 1
    cq, ck, cv, cg = 4, 5, 6, 13

    def cur(col):
        return pl.BlockSpec((None, tm, COL), lambda b, i: (b, i, col))

    def prev(col):
        return pl.BlockSpec((None, HALO, COL), lambda b, i: (b, jnp.maximum(i * hb - 1, 0), col))

    def nxt(col):
        return pl.BlockSpec((None, HALO, COL), lambda b, i: (b, jnp.minimum((i + 1) * hb, last_halo), col))

    tok_spec = pl.BlockSpec((None, tm, COL), lambda b, i: (b, i, 0))
    lse_spec = pl.BlockSpec((None, PAIRS, tm, LANES), lambda b, i: (b, 0, i, 0))
    return pl.pallas_call(
        functools.partial(_band_merge_kernel, tm=tm, tiles=tiles),
        grid=(batch, tiles),
        in_specs=[cur(cq), prev(ck), cur(ck), nxt(ck), prev(cv), cur(cv), nxt(cv),
                  pl.BlockSpec(bias_tab.shape, lambda b, i: (0, 0, 0)),
                  tok_spec, lse_spec, tok_spec, lse_spec, cur(cg)],
        out_specs=tok_spec,
        out_shape=jax.ShapeDtypeStruct((batch, seq_len, COL), BF16),
        compiler_params=pltpu.CompilerParams(
            dimension_semantics=("parallel", "arbitrary"), vmem_limit_bytes=VMEM_LIMIT),
        name="band_merge",
    )(*([proj3] * 7), bias_tab, o1, l1, o2, l2, proj3)


def _out_kernel(x_ref, ua_ref, ub_ref, sa_ref, sb_ref, wa_ref, wb_ref, wo_ref, y_ref):
    br_a = jnp.dot(ua_ref[...], wa_ref[...], preferred_element_type=F32)
    br_b = jnp.dot(ub_ref[...], wb_ref[...], preferred_element_type=F32)
    merged = sa_ref[...].astype(F32) * br_a + sb_ref[...].astype(F32) * br_b
    y_ref[...] = x_ref[...] + jnp.dot(merged.astype(BF16), wo_ref[...], preferred_element_type=F32)


def _output(x2d, ua, ub, proj, wa, wb, wo, tm=1024):
    n_tok = x2d.shape[0]
    sig_a = (14 * COL) // D_MODEL
    sig_b = (16 * COL) // D_MODEL
    return pl.pallas_call(
        _out_kernel,
        grid=(n_tok // tm,),
        in_specs=[
            pl.BlockSpec((tm, D_MODEL), lambda i: (i, 0)),
            pl.BlockSpec((tm, COL), lambda i: (i, 0)),
            pl.BlockSpec((tm, COL), lambda i: (i, 0)),
            pl.BlockSpec((tm, D_MODEL), lambda i: (i, sig_a)),
            pl.BlockSpec((tm, D_MODEL), lambda i: (i, sig_b)),
            pl.BlockSpec((COL, D_MODEL), lambda i: (0, 0)),
            pl.BlockSpec((COL, D_MODEL), lambda i: (0, 0)),
            pl.BlockSpec((D_MODEL, D_MODEL), lambda i: (0, 0)),
        ],
        out_specs=pl.BlockSpec((tm, D_MODEL), lambda i: (i, 0)),
        out_shape=jax.ShapeDtypeStruct((n_tok, D_MODEL), F32),
        compiler_params=pltpu.CompilerParams(
            dimension_semantics=("parallel",), vmem_limit_bytes=VMEM_LIMIT),
        name="out_proj",
    )(x2d, ua, ub, proj, proj, wa, wb, wo)


def _rope_tables(seq_len):
    inv = ROPE_THETA ** (-jnp.arange(0, HEAD_DIM, 2, dtype=F32) / HEAD_DIM)
    ang = jnp.arange(seq_len, dtype=F32)[:, None] * inv[None, :]
    cos = jnp.cos(ang)
    sin = jnp.sin(ang)
    reps = LANES // HEAD_DIM
    cos_t = jnp.concatenate([cos, cos] * reps, axis=1)
    sin_t = jnp.concatenate([-sin, sin] * reps, axis=1)
    return cos_t, sin_t


def _tile_gains(qn_a, kn_a, qn_b, kn_b):
    scale = LOG2_E / math.sqrt(HEAD_DIM)
    heads = COL // HEAD_DIM
    rows = []
    for t in range(N_COL_TILES):
        if t == 0:
            g = qn_a
        elif t == 1:
            g = kn_a
        elif t in (4, 7, 10):
            g = qn_b
        elif t in (5, 8, 11):
            g = kn_b
        else:
            g = jnp.ones((HEAD_DIM,), F32)
        g = g.astype(F32)
        if t in _Q_TILES:
            g = g * scale
        rows.append(jnp.tile(g, heads))
    return jnp.stack(rows)[:, None, :]


def _layer(x, ng, w_bf, gains, bias_tab, bd, cos, sin, wa, wb, wo):
    batch, seq_len, _ = x.shape
    assert seq_len % TILE == 0 and seq_len <= cos.shape[0]
    x2d = x.reshape(batch * seq_len, D_MODEL)
    proj = _project(x2d, seq_len, ng, w_bf, gains, cos, sin, bd)
    proj3 = proj.reshape(batch, seq_len, D_IN)
    ua = _neighbourhood(proj3, bias_tab)
    band_tab = _band_bias_table()
    o2, l2 = _band_dilated(proj3, band_tab, 2, DIL_GROUPS[2][1])
    o1, l1 = _band_dilated(proj3, band_tab, 1, DIL_GROUPS[1][1])
    ub = _band_merge(proj3, band_tab, o1, l1, o2, l2)
    y = _output(x2d, ua.reshape(-1, COL), ub.reshape(-1, COL), proj, wa, wb, wo)
    return y.reshape(batch, seq_len, D_MODEL)


def kernel(x_prompt, x_sample, norm_gain, w_in, qn_a, kn_a, rel_bias_a, qn_b, kn_b,
           w_branch_a, w_branch_b, w_out):
    depth = norm_gain.shape[0]
    blk = jnp.arange(256) // HEAD_DIM
    bd = jnp.where(blk[:, None] == blk[None, :], 1.0 / HEAD_DIM, 0.0).astype(BF16)
    cos, sin = _rope_tables(max(x_prompt.shape[1], x_sample.shape[1]))
    y_prompt, y_sample = x_prompt, x_sample
    for l in range(depth):
        ng = norm_gain[l].astype(F32)[None, :]
        w_bf = w_in[l].astype(BF16)
        gains = _tile_gains(qn_a[l], kn_a[l], qn_b[l], kn_b[l])
        bias_tab = _na_bias_table(rel_bias_a[l])
        wa = w_branch_a[l].astype(BF16)
        wb = w_branch_b[l].astype(BF16)
        wo = w_out[l].astype(BF16)
        y_prompt = _layer(y_prompt, ng, w_bf, gains, bias_tab, bd, cos, sin, wa, wb, wo)
        y_sample = _layer(y_sample, ng, w_bf, gains, bias_tab, bd, cos, sin, wa, wb, wo)
    return (y_prompt, y_sample)
```

```python
import functools
import math

import jax
import jax.numpy as jnp
from jax import lax
from jax.experimental import pallas as pl
from jax.experimental.pallas import tpu as pltpu

F32 = jnp.float32
BF16 = jnp.bfloat16

D_MODEL = 1024
HEAD_DIM = 64
GRID_W = 64
NA_HEADS = 8
NA_ROWS = 8
NA_COLS = 16
DIL_GROUPS = ((128, 1), (512, 4), (2048, 16))
ROPE_THETA = 10000.0
EPS = 1e-6
NEG = -1e30
LOG2_E = math.log2(math.e)

COL = 512
N_COL_TILES = 18
D_IN = COL * N_COL_TILES
LANES = 128
PAIRS = COL // LANES
HALO = 64
TILE = 2048
BLOCKS = TILE // HALO
TQ = 2 * HALO
NORM_CHUNK = 512
PLAIN_CHUNK = 256

_NORM_TILES = (0, 1)
_PLAIN_TILES = (2, 6)
_DILATED_TILES = {1: (4, 5, 6), 4: (7, 8, 9), 16: (10, 11, 12)}
_SILU_TILES = (3, 13)
_SIGM_TILES = (14, 15, 16, 17)
_Q_TILES = (0, 4, 7, 10)

VMEM_LIMIT = 56 * 1024 * 1024


def _any_of(j, members):
    return functools.reduce(jnp.logical_or, [j == m for m in members])


def _proj_kernel(x_ref, ng_ref, w_ref, gain_ref, cos_ref, sin_ref, bd_ref, o_ref, h_ref, perm_ref):
    j = pl.program_id(1)
    lane = lax.broadcasted_iota(jnp.int32, (1, LANES), 1)
    first_half = (lane % HEAD_DIM) < (HEAD_DIM // 2)

    def normalize_rows(r0, rc):
        x = x_ref[r0:r0 + rc, :]
        ms = jnp.mean(x * x, axis=-1, keepdims=True)
        h_ref[r0:r0 + rc, :] = (x * lax.rsqrt(ms + EPS) * ng_ref[...]).astype(BF16)

    def head_norm(a, r0, rc):
        sq = (a * a).astype(BF16)
        bd = bd_ref[...]
        ms = jnp.concatenate(
            [jnp.dot(sq[:, c:c + 256], bd, preferred_element_type=F32) for c in (0, 256)], axis=1)
        return a * lax.rsqrt(ms + EPS) * gain_ref[0]

    def rope(a, r0, rc):
        y = head_norm(a, r0, rc)
        cos = cos_ref[r0:r0 + rc, :]
        sin = sin_ref[r0:r0 + rc, :]
        parts = []
        for c in range(0, COL, LANES):
            yc = y[:, c:c + LANES]
            partner = jnp.where(first_half,
                                pltpu.roll(yc, LANES - HEAD_DIM // 2, 1),
                                pltpu.roll(yc, HEAD_DIM // 2, 1))
            parts.append(yc * cos + partner * sin)
        return jnp.concatenate(parts, axis=1)

    def sigmoid(a):
        return 0.5 * jnp.tanh(0.5 * a) + 0.5

    def run(members, epilogue, rc, dil=None, first=False):
        @pl.when(_any_of(j, members))
        def _():
            for r0 in range(0, TILE, rc):
                if first:
                    normalize_rows(r0, rc)
                a = jnp.dot(h_ref[r0:r0 + rc, :], w_ref[...], preferred_element_type=F32)
                y = epilogue(a, r0, rc)
                if dil is None:
                    o_ref[r0:r0 + rc, :] = y.astype(BF16)
                    continue
                for cp in range(PAIRS):
                    perm_ref[cp, r0:r0 + rc, :] = y[:, cp * LANES:(cp + 1) * LANES]
                span = HALO * dil
                done = r0 + rc
                for c in range(r0 // span, done // span):
                    for cp in range(PAIRS):
                        for ph in range(dil):
                            blk = c * dil + ph
                            rows = perm_ref[cp, pl.ds(c * span + ph, HALO, stride=dil), :]
                            o_ref[blk * HALO:(blk + 1) * HALO, cp * LANES:(cp + 1) * LANES] = rows.astype(BF16)

    plain = lambda a, r0, rc: a
    assert _NORM_TILES[0] == 0
    run(_NORM_TILES[:1], head_norm, NORM_CHUNK, first=True)
    run(_NORM_TILES[1:], head_norm, NORM_CHUNK)
    run(_PLAIN_TILES, plain, PLAIN_CHUNK)
    run(_SILU_TILES, lambda a, r0, rc: a * sigmoid(a), PLAIN_CHUNK)
    run(_SIGM_TILES, lambda a, r0, rc: sigmoid(a), PLAIN_CHUNK)
    for dil, (q_tile, k_tile, v_tile) in _DILATED_TILES.items():
        run((q_tile, k_tile), rope, NORM_CHUNK, None if dil == 1 else dil)
        if dil != 1:
            run((v_tile,), plain, PLAIN_CHUNK, dil)


def _project(x2d, seq_len, ng, w_bf, gains, cos, sin, bd):
    n_tok = x2d.shape[0]
    pos_blocks = seq_len // TILE
    return pl.pallas_call(
        _proj_kernel,
        grid=(n_tok // TILE, N_COL_TILES),
        in_specs=[
            pl.BlockSpec((TILE, D_MODEL), lambda i, j: (i, 0)),
            pl.BlockSpec((1, D_MODEL), lambda i, j: (0, 0)),
            pl.BlockSpec((D_MODEL, COL), lambda i, j: (0, j)),
            pl.BlockSpec((1, 1, COL), lambda i, j: (j, 0, 0)),
            pl.BlockSpec((TILE, LANES), lambda i, j: (i % pos_blocks, 0)),
            pl.BlockSpec((TILE, LANES), lambda i, j: (i % pos_blocks, 0)),
            pl.BlockSpec((256, 256), lambda i, j: (0, 0)),
        ],
        out_specs=pl.BlockSpec((TILE, COL), lambda i, j: (i, j)),
        out_shape=jax.ShapeDtypeStruct((n_tok, D_IN), BF16),
        scratch_shapes=[pltpu.VMEM((TILE, D_MODEL), BF16), pltpu.VMEM((PAIRS, TILE, LANES), F32)],
        compiler_params=pltpu.CompilerParams(
            dimension_semantics=("parallel", "arbitrary"), vmem_limit_bytes=VMEM_LIMIT),
        name="in_proj",
    )(x2d, ng, w_bf, gains, cos, sin, bd)


def _lane_lo():
    return lax.broadcasted_iota(jnp.int32, (1, LANES), 1) < HEAD_DIM


def _attend_pipelined(units, lo, need_lse):
    n = len(units)
    live = {}

    def scores(t):
        q, k = units[t][0]()
        zero = jnp.zeros_like(q)
        qq = jnp.concatenate([jnp.where(lo, q, zero), jnp.where(lo, zero, q)], axis=0)
        s = lax.dot_general(qq, k, (((1,), (1,)), ((), ())), preferred_element_type=F32) + units[t][1]()
        live[t] = (s, jnp.max(s, axis=-1, keepdims=True))

    def numerator(t):
        s, m = live.pop(t)
        p = jnp.exp2(s - m)
        l = jnp.sum(p, axis=-1, keepdims=True)
        live[t] = (p.astype(BF16), l, m + jnp.log2(l) if need_lse else None)

    def values(t):
        p, l, lse2 = live.pop(t)
        v = units[t][2]()
        o2 = jnp.dot(p, v, preferred_element_type=F32) / l
        nq = o2.shape[0] // 2
        lse = None
        if need_lse:
            lse = jnp.where(lo, lse2[:nq], lse2[nq:])
        units[t][3](jnp.where(lo, o2[:nq], o2[nq:]), lse)

    for t in range(n + 2):
        if t < n:
            scores(t)
        if 1 <= t <= n:
            numerator(t - 1)
        if t >= 2:
            values(t - 2)


def _na_kernel(q_ref, k_ref, v_ref, g_ref, bias_ref, o_ref, *, rows, rq):
    rb = pl.program_id(2)
    lo = _lane_lo()
    kwin = NA_ROWS * GRID_W

    def unit(rr):
        r = rb * rq + rr
        rs = jnp.clip(r - NA_ROWS // 2, 0, rows - NA_ROWS)
        shift = rs - r + (NA_ROWS - 1)
        window = pl.ds(pl.multiple_of(rs * GRID_W, GRID_W), kwin)
        rows_q = slice(rr * GRID_W, (rr + 1) * GRID_W)

        def load_bias():
            return jnp.concatenate([bias_ref[shift + 2 * i] for i in range(NA_ROWS // 2)], axis=1)

        def finish(o, lse):
            o_ref[rows_q, :] = (o * g_ref[rows_q, :].astype(F32)).astype(BF16)

        return (lambda: (q_ref[rows_q, :], k_ref[window, :]), load_bias, lambda: v_ref[window, :], finish)

    _attend_pipelined([unit(rr) for rr in range(rq)], lo, need_lse=False)


def _neighbourhood(proj3, bias_tab, rq=64):
    batch, seq_len, _ = proj3.shape
    rows = seq_len // GRID_W
    tq = rq * GRID_W
    return pl.pallas_call(
        functools.partial(_na_kernel, rows=rows, rq=rq),
        grid=(batch, PAIRS, rows // rq),
        in_specs=[
            pl.BlockSpec((None, tq, LANES), lambda b, hp, rb: (b, rb, 0 * PAIRS + hp)),
            pl.BlockSpec((None, seq_len, LANES), lambda b, hp, rb: (b, 0, 1 * PAIRS + hp)),
            pl.BlockSpec((None, seq_len, LANES), lambda b, hp, rb: (b, 0, 2 * PAIRS + hp)),
            pl.BlockSpec((None, tq, LANES), lambda b, hp, rb: (b, rb, 3 * PAIRS + hp)),
            pl.BlockSpec((None, 2 * NA_ROWS - 2, 2 * GRID_W, 2 * GRID_W), lambda b, hp, rb: (hp, 0, 0, 0)),
        ],
        out_specs=pl.BlockSpec((None, tq, LANES), lambda b, hp, rb: (b, rb, hp)),
        out_shape=jax.ShapeDtypeStruct((batch, seq_len, COL), BF16),
        compiler_params=pltpu.CompilerParams(
            dimension_semantics=("parallel", "parallel", "arbitrary"), vmem_limit_bytes=VMEM_LIMIT),
        name="na_attn",
    )(proj3, proj3, proj3, proj3, bias_tab)


def _na_bias_table(rel_bias):
    heads, n_dr, n_dc = rel_bias.shape
    period = 2 * GRID_W
    diag = jnp.pad(rel_bias.astype(F32), ((0, 0), (0, 0), (0, period - n_dc)))
    diag = jnp.roll(diag, -(NA_COLS - 1), axis=-1)
    tiled = jnp.tile(diag, (1, 1, GRID_W))[:, :, :GRID_W * (period - 1)]
    toeplitz = tiled.reshape(heads, n_dr, GRID_W, period - 1)[..., :GRID_W]
    c = jnp.arange(GRID_W)
    cs = jnp.clip(c - NA_COLS // 2, 0, GRID_W - NA_COLS)
    ok = (c[None, :] >= cs[:, None]) & (c[None, :] < cs[:, None] + NA_COLS)
    toeplitz = jnp.where(ok[None, None], toeplitz * LOG2_E, NEG)
    pairs = jnp.concatenate([toeplitz[:, :n_dr - 1], toeplitz[:, 1:]], axis=-1)
    pairs = pairs.reshape(heads // 2, 2, n_dr - 1, GRID_W, 2 * GRID_W).transpose(0, 2, 1, 3, 4)
    return pairs.reshape(heads // 2, n_dr - 1, 2 * GRID_W, 2 * GRID_W)


def _band_bias_table():
    nk = TQ + 2 * HALO
    qpos = jnp.arange(2 * TQ)[:, None] % TQ
    col = jnp.arange(nk)[None, :]
    band = jnp.abs(col - HALO - qpos) <= HALO
    masks = [band & ((col >= HALO) | (not first)) & ((col < TQ + HALO) | (not last))
             for last in (False, True) for first in (False, True)]
    return jnp.where(jnp.stack(masks), 0.0, NEG).astype(F32)


def _band_bias_index(at_start, at_end):
    idx = 0
    if at_start is not None:
        idx = idx + at_start.astype(jnp.int32)
    if at_end is not None:
        idx = idx + 2 * at_end.astype(jnp.int32)
    return idx


def _band_dilated_kernel(q_ref, kp_ref, kc_ref, kn_ref, vp_ref, vc_ref, vn_ref, bias_ref, out_ref, lse_ref,
                         o_scr, *, dil, tiles):
    i = pl.program_id(1)
    lo = _lane_lo()
    ch = TQ // HALO
    n_units = BLOCKS // dil // ch
    first, last = i == 0, i == tiles - 1

    def rows(chunk, ph):
        return slice((chunk * dil + ph) * HALO, (chunk * dil + ph + 1) * HALO)

    def unit(ph, a, hp):
        sl = slice(hp * LANES, (hp + 1) * LANES)
        halo = slice(ph * HALO, (ph + 1) * HALO)
        bias_idx = _band_bias_index(first if a == 0 else None, last if a == n_units - 1 else None)

        def chunks(ref):
            return [ref[rows(c, ph), sl] for c in range(a * ch, (a + 1) * ch)]

        def window(prev_ref, cur_ref, next_ref):
            head = prev_ref[halo, sl] if a == 0 else cur_ref[rows(a * ch - 1, ph), sl]
            tail = next_ref[halo, sl] if a == n_units - 1 else cur_ref[rows((a + 1) * ch, ph), sl]
            return jnp.concatenate([head] + chunks(cur_ref) + [tail], axis=0)

        def load_qk():
            return jnp.concatenate(chunks(q_ref), axis=0), window(kp_ref, kc_ref, kn_ref)

        def finish(o, lse):
            tok = pl.ds(a * TQ * dil + ph, TQ, stride=dil)
            o_scr[hp, tok, :] = o
            lse_ref[hp, tok, :] = lse

        return load_qk, lambda: bias_ref[bias_idx], lambda: window(vp_ref, vc_ref, vn_ref), finish

    units = [unit(ph, a, hp) for ph in range(dil) for a in range(n_units) for hp in range(PAIRS)]
    _attend_pipelined(units, lo, need_lse=True)
    for hp in range(PAIRS):
        out_ref[:, hp * LANES:(hp + 1) * LANES] = o_scr[hp].astype(BF16)


def _band_dilated(proj3, bias_tab, group, dil):
    batch, seq_len, _ = proj3.shape
    tiles = seq_len // TILE
    cq, ck, cv = 4 + 3 * group, 5 + 3 * group, 6 + 3 * group
    halo_rows = dil * HALO
    per_tile = TILE // halo_rows
    last_unit = seq_len // halo_rows - 1

    def cur(col):
        return pl.BlockSpec((None, TILE, COL), lambda b, i: (b, i, col))

    def prev(col):
        return pl.BlockSpec((None, halo_rows, COL), lambda b, i: (b, jnp.maximum(i * per_tile - 1, 0), col))

    def nxt(col):
        return pl.BlockSpec((None, halo_rows, COL),
                            lambda b, i: (b, jnp.minimum((i + 1) * per_tile, last_unit), col))

    return pl.pallas_call(
        functools.partial(_band_dilated_kernel, dil=dil, tiles=tiles),
        grid=(batch, tiles),
        in_specs=[cur(cq), prev(ck), cur(ck), nxt(ck), prev(cv), cur(cv), nxt(cv),
                  pl.BlockSpec(bias_tab.shape, lambda b, i: (0, 0, 0))],
        out_specs=[pl.BlockSpec((None, TILE, COL), lambda b, i: (b, i, 0)),
                   pl.BlockSpec((None, PAIRS, TILE, LANES), lambda b, i: (b, 0, i, 0))],
        out_shape=[jax.ShapeDtypeStruct((batch, seq_len, COL), BF16),
                   jax.ShapeDtypeStruct((batch, PAIRS, seq_len, LANES), F32)],
        scratch_shapes=[pltpu.VMEM((PAIRS, TILE, LANES), F32)],
        compiler_params=pltpu.CompilerParams(
            dimension_semantics=("parallel", "arbitrary"), vmem_limit_bytes=VMEM_LIMIT),
        name="band_d%d" % dil,
    )(*([proj3] * 7), bias_tab)


def _band_merge_kernel(q_ref, kp_ref, kc_ref, kn_ref, vp_ref, vc_ref, vn_ref, bias_ref,
                       o1_ref, l1_ref, o2_ref, l2_ref, g_ref, out_ref, *, tm, tiles):
    i = pl.program_id(1)
    lo = _lane_lo()
    n_units = tm // TQ
    first, last = i == 0, i == tiles - 1

    def unit(a, hp):
        sl = slice(hp * LANES, (hp + 1) * LANES)
        qrows = slice(a * TQ, (a + 1) * TQ)
        bias_idx = _band_bias_index(first if a == 0 else None, last if a == n_units - 1 else None)

        def window(prev_ref, cur_ref, next_ref):
            if a == 0:
                return jnp.concatenate([prev_ref[:, sl], cur_ref[0:TQ + HALO, sl]], axis=0)
            if a == n_units - 1:
                return jnp.concatenate([cur_ref[tm - TQ - HALO:tm, sl], next_ref[:, sl]], axis=0)
            return cur_ref[a * TQ - HALO:(a + 1) * TQ + HALO, sl]

        def finish(o, lse):
            l1 = l1_ref[hp, qrows, :]
            l2 = l2_ref[hp, qrows, :]
            top = jnp.maximum(lse, jnp.maximum(l1, l2))
            w0 = jnp.exp2(lse - top)
            w1 = jnp.exp2(l1 - top)
            w2 = jnp.exp2(l2 - top)
            num = w0 * o + w1 * o1_ref[qrows, sl].astype(F32) + w2 * o2_ref[qrows, sl].astype(F32)
            ob = num / (w0 + w1 + w2)
            out_ref[qrows, sl] = (ob * g_ref[qrows, sl].astype(F32)).astype(BF16)

        return (lambda: (q_ref[qrows, sl], window(kp_ref, kc_ref, kn_ref)),
                lambda: bias_ref[bias_idx],
                lambda: window(vp_ref, vc_ref, vn_ref),
                finish)

    units = [unit(a, hp) for a in range(n_units) for hp in range(PAIRS)]
    _attend_pipelined(units, lo, need_lse=True)


def _band_merge(proj3, bias_tab, o1, l1, o2, l2, tm=2048):
    batch, seq_len, _ = proj3.shape
    tiles = seq_len // tm
    hb = tm // HALO
    last_halo = seq_len // HALO - 1
    cq, ck, cv, cg = 4, 5, 6, 13

    def cur(col):
        return pl.BlockSpec((None, tm, COL), lambda b, i: (b, i, col))

    def prev(col):
        return pl.BlockSpec((None, HALO, COL), lambda b, i: (b, jnp.maximum(i * hb - 1, 0), col))

    def nxt(col):
        return pl.BlockSpec((None, HALO, COL), lambda b, i: (b, jnp.minimum((i + 1) * hb, last_halo), col))

    tok_spec = pl.BlockSpec((None, tm, COL), lambda b, i: (b, i, 0))
    lse_spec = pl.BlockSpec((None, PAIRS, tm, LANES), lambda b, i: (b, 0, i, 0))
    return pl.pallas_call(
        functools.partial(_band_merge_kernel, tm=tm, tiles=tiles),
        grid=(batch, tiles),
        in_specs=[cur(cq), prev(ck), cur(ck), nxt(ck), prev(cv), cur(cv), nxt(cv),
                  pl.BlockSpec(bias_tab.shape, lambda b, i: (0, 0, 0)),
                  tok_spec, lse_spec, tok_spec, lse_spec, cur(cg)],
        out_specs=tok_spec,
        out_shape=jax.ShapeDtypeStruct((batch, seq_len, COL), BF16),
        compiler_params=pltpu.CompilerParams(
            dimension_semantics=("parallel", "arbitrary"), vmem_limit_bytes=VMEM_LIMIT),
        name="band_merge",
    )(*([proj3] * 7), bias_tab, o1, l1, o2, l2, proj3)


def _out_kernel(x_ref, ua_ref, ub_ref, sa_ref, sb_ref, wa_ref, wb_ref, wo_ref, y_ref):
    br_a = jnp.dot(ua_ref[...], wa_ref[...], preferred_element_type=F32)
    br_b = jnp.dot(ub_ref[...], wb_ref[...], preferred_element_type=F32)
    merged = sa_ref[...].astype(F32) * br_a + sb_ref[...].astype(F32) * br_b
    y_ref[...] = x_ref[...] + jnp.dot(merged.astype(BF16), wo_ref[...], preferred_element_type=F32)


def _output(x2d, ua, ub, proj, wa, wb, wo, tm=1024):
    n_tok = x2d.shape[0]
    sig_a = (14 * COL) // D_MODEL
    sig_b = (16 * COL) // D_MODEL
    return pl.pallas_call(
        _out_kernel,
        grid=(n_tok // tm,),
        in_specs=[
            pl.BlockSpec((tm, D_MODEL), lambda i: (i, 0)),
            pl.BlockSpec((tm, COL), lambda i: (i, 0)),
            pl.BlockSpec((tm, COL), lambda i: (i, 0)),
            pl.BlockSpec((tm, D_MODEL), lambda i: (i, sig_a)),
            pl.BlockSpec((tm, D_MODEL), lambda i: (i, sig_b)),
            pl.BlockSpec((COL, D_MODEL), lambda i: (0, 0)),
            pl.BlockSpec((COL, D_MODEL), lambda i: (0, 0)),
            pl.BlockSpec((D_MODEL, D_MODEL), lambda i: (0, 0)),
        ],
        out_specs=pl.BlockSpec((tm, D_MODEL), lambda i: (i, 0)),
        out_shape=jax.ShapeDtypeStruct((n_tok, D_MODEL), F32),
        compiler_params=pltpu.CompilerParams(
            dimension_semantics=("parallel",), vmem_limit_bytes=VMEM_LIMIT),
        name="out_proj",
    )(x2d, ua, ub, proj, proj, wa, wb, wo)


def _rope_tables(seq_len):
    inv = ROPE_THETA ** (-jnp.arange(0, HEAD_DIM, 2, dtype=F32) / HEAD_DIM)
    ang = jnp.arange(seq_len, dtype=F32)[:, None] * inv[None, :]
    cos, sin = lax.optimization_barrier((jnp.cos(ang), jnp.sin(ang)))
    reps = LANES // HEAD_DIM
    cos_t = jnp.concatenate([cos, cos] * reps, axis=1)
    sin_t = jnp.concatenate([-sin, sin] * reps, axis=1)
    return cos_t, sin_t


def _tile_gains(qn_a, kn_a, qn_b, kn_b):
    scale = LOG2_E / math.sqrt(HEAD_DIM)
    heads = COL // HEAD_DIM
    rows = []
    for t in range(N_COL_TILES):
        if t == 0:
            g = qn_a
        elif t == 1:
            g = kn_a
        elif t in (4, 7, 10):
            g = qn_b
        elif t in (5, 8, 11):
            g = kn_b
        else:
            g = jnp.ones((HEAD_DIM,), F32)
        g = g.astype(F32)
        if t in _Q_TILES:
            g = g * scale
        rows.append(jnp.tile(g, heads))
    return jnp.stack(rows)[:, None, :]


def _layer(x, ng, w_bf, gains, bias_tab, bd, cos, sin, wa, wb, wo):
    batch, seq_len, _ = x.shape
    assert seq_len % TILE == 0 and seq_len <= cos.shape[0]
    x2d = x.reshape(batch * seq_len, D_MODEL)
    proj = _project(x2d, seq_len, ng, w_bf, gains, cos, sin, bd)
    proj3 = proj.reshape(batch, seq_len, D_IN)
    ua = _neighbourhood(proj3, bias_tab)
    band_tab = _band_bias_table()
    o2, l2 = _band_dilated(proj3, band_tab, 2, DIL_GROUPS[2][1])
    o1, l1 = _band_dilated(proj3, band_tab, 1, DIL_GROUPS[1][1])
    ub = _band_merge(proj3, band_tab, o1, l1, o2, l2)
    y = _output(x2d, ua.reshape(-1, COL), ub.reshape(-1, COL), proj, wa, wb, wo)
    return y.reshape(batch, seq_len, D_MODEL)


def kernel(x_prompt, x_sample, norm_gain, w_in, qn_a, kn_a, rel_bias_a, qn_b, kn_b,
           w_branch_a, w_branch_b, w_out):
    depth = norm_gain.shape[0]
    blk = jnp.arange(256) // HEAD_DIM
    bd = jnp.where(blk[:, None] == blk[None, :], 1.0 / HEAD_DIM, 0.0).astype(BF16)
    cos, sin = _rope_tables(max(x_prompt.shape[1], x_sample.shape[1]))
    y_prompt, y_sample = x_prompt, x_sample
    for l in range(depth):
        ng = norm_gain[l].astype(F32)[None, :]
        w_bf = w_in[l].astype(BF16)
        gains = _tile_gains(qn_a[l], kn_a[l], qn_b[l], kn_b[l])
        bias_tab = _na_bias_table(rel_bias_a[l])
        wa = w_branch_a[l].astype(BF16)
        wb = w_branch_b[l].astype(BF16)
        wo = w_out[l].astype(BF16)
        y_prompt = _layer(y_prompt, ng, w_bf, gains, bias_tab, bd, cos, sin, wa, wb, wo)
        y_sample = _layer(y_sample, ng, w_bf, gains, bias_tab, bd, cos, sin, wa, wb, wo)
    return (y_prompt, y_sample)
```

```python
import functools
import math

import jax
import jax.numpy as jnp
from jax import lax
from jax.experimental import pallas as pl
from jax.experimental.pallas import tpu as pltpu

F32 = jnp.float32
BF16 = jnp.bfloat16

D_MODEL = 1024
HEAD_DIM = 64
GRID_W = 64
NA_HEADS = 8
NA_ROWS = 8
NA_COLS = 16
DIL_GROUPS = ((128, 1), (512, 4), (2048, 16))
ROPE_THETA = 10000.0
EPS = 1e-6
NEG = -1e30
LOG2_E = math.log2(math.e)

COL = 512
N_COL_TILES = 18
D_IN = COL * N_COL_TILES
LANES = 128
PAIRS = COL // LANES
HALO = 64
TILE = 2048
BLOCKS = TILE // HALO
TQ = 2 * HALO
NORM_CHUNK = 512
PLAIN_CHUNK = 256

_NORM_TILES = (0, 1)
_PLAIN_TILES = (2, 6)
_DILATED_TILES = {1: (4, 5, 6), 4: (7, 8, 9), 16: (10, 11, 12)}
_SILU_TILES = (3, 13)
_SIGM_TILES = (14, 15, 16, 17)
_Q_TILES = (0, 4, 7, 10)

VMEM_LIMIT = 56 * 1024 * 1024


def _any_of(j, members):
    return functools.reduce(jnp.logical_or, [j == m for m in members])


def _proj_kernel(x_ref, ng_ref, w_ref, gain_ref, cos_ref, sin_ref, bd_ref, o_ref, h_ref, perm_ref):
    j = pl.program_id(1)
    lane = lax.broadcasted_iota(jnp.int32, (1, LANES), 1)
    first_half = (lane % HEAD_DIM) < (HEAD_DIM // 2)

    def normalize_rows(r0, rc):
        x = x_ref[r0:r0 + rc, :]
        ms = jnp.mean(x * x, axis=-1, keepdims=True)
        h_ref[r0:r0 + rc, :] = (x * lax.rsqrt(ms + EPS) * ng_ref[...]).astype(BF16)

    def head_norm(a, r0, rc):
        sq = (a * a).astype(BF16)
        bd = bd_ref[...]
        ms = jnp.concatenate(
            [jnp.dot(sq[:, c:c + 256], bd, preferred_element_type=F32) for c in (0, 256)], axis=1)
        return a * lax.rsqrt(ms + EPS) * gain_ref[0]

    def rope(a, r0, rc):
        y = head_norm(a, r0, rc)
        cos = cos_ref[r0:r0 + rc, :]
        sin = sin_ref[r0:r0 + rc, :]
        parts = []
        for c in range(0, COL, LANES):
            yc = y[:, c:c + LANES]
            partner = jnp.where(first_half,
                                pltpu.roll(yc, LANES - HEAD_DIM // 2, 1),
                                pltpu.roll(yc, HEAD_DIM // 2, 1))
            parts.append(yc * cos + partner * sin)
        return jnp.concatenate(parts, axis=1)

    def sigmoid(a):
        return 0.5 * jnp.tanh(0.5 * a) + 0.5

    def run(members, epilogue, rc, dil=None, first=False):
        @pl.when(_any_of(j, members))
        def _():
            for r0 in range(0, TILE, rc):
                if first:
                    normalize_rows(r0, rc)
                a = jnp.dot(h_ref[r0:r0 + rc, :], w_ref[...], preferred_element_type=F32)
                y = epilogue(a, r0, rc)
                if dil is None:
                    o_ref[r0:r0 + rc, :] = y.astype(BF16)
                    continue
                for cp in range(PAIRS):
                    perm_ref[cp, r0:r0 + rc, :] = y[:, cp * LANES:(cp + 1) * LANES]
                span = HALO * dil
                done = r0 + rc
                for c in range(r0 // span, done // span):
                    for cp in range(PAIRS):
                        for ph in range(dil):
                            blk = c * dil + ph
                            rows = perm_ref[cp, pl.ds(c * span + ph, HALO, stride=dil), :]
                            o_ref[blk * HALO:(blk + 1) * HALO, cp * LANES:(cp + 1) * LANES] = rows.astype(BF16)

    plain = lambda a, r0, rc: a
    assert _NORM_TILES[0] == 0
    run(_NORM_TILES[:1], head_norm, NORM_CHUNK, first=True)
    run(_NORM_TILES[1:], head_norm, NORM_CHUNK)
    run(_PLAIN_TILES, plain, PLAIN_CHUNK)
    run(_SILU_TILES, lambda a, r0, rc: a * sigmoid(a), PLAIN_CHUNK)
    run(_SIGM_TILES, lambda a, r0, rc: sigmoid(a), PLAIN_CHUNK)
    for dil, (q_tile, k_tile, v_tile) in _DILATED_TILES.items():
        run((q_tile, k_tile), rope, NORM_CHUNK, None if dil == 1 else dil)
        if dil != 1:
            run((v_tile,), plain, PLAIN_CHUNK, dil)


def _project(x2d, seq_len, ng, w_bf, gains, cos, sin, bd):
    n_tok = x2d.shape[0]
    pos_blocks = seq_len // TILE
    return pl.pallas_call(
        _proj_kernel,
        grid=(n_tok // TILE, N_COL_TILES),
        in_specs=[
            pl.BlockSpec((TILE, D_MODEL), lambda i, j: (i, 0)),
            pl.BlockSpec((1, D_MODEL), lambda i, j: (0, 0)),
            pl.BlockSpec((D_MODEL, COL), lambda i, j: (0, j)),
            pl.BlockSpec((1, 1, COL), lambda i, j: (j, 0, 0)),
            pl.BlockSpec((TILE, LANES), lambda i, j: (i % pos_blocks, 0)),
            pl.BlockSpec((TILE, LANES), lambda i, j: (i % pos_blocks, 0)),
            pl.BlockSpec((256, 256), lambda i, j: (0, 0)),
        ],
        out_specs=pl.BlockSpec((TILE, COL), lambda i, j: (i, j)),
        out_shape=jax.ShapeDtypeStruct((n_tok, D_IN), BF16),
        scratch_shapes=[pltpu.VMEM((TILE, D_MODEL), BF16), pltpu.VMEM((PAIRS, TILE, LANES), F32)],
        compiler_params=pltpu.CompilerParams(
            dimension_semantics=("parallel", "arbitrary"), vmem_limit_bytes=VMEM_LIMIT),
        name="in_proj",
    )(x2d, ng, w_bf, gains, cos, sin, bd)


def _lane_lo():
    return lax.broadcasted_iota(jnp.int32, (1, LANES), 1) < HEAD_DIM


def _attend_pipelined(units, lo, need_lse):
    n = len(units)
    live = {}

    def scores(t):
        q, k = units[t][0]()
        zero = jnp.zeros_like(q)
        qq = jnp.concatenate([jnp.where(lo, q, zero), jnp.where(lo, zero, q)], axis=0)
        s = lax.dot_general(qq, k, (((1,), (1,)), ((), ())), preferred_element_type=F32) + units[t][1]()
        live[t] = (s, jnp.max(s, axis=-1, keepdims=True))

    def numerator(t):
        s, m = live.pop(t)
        p = jnp.exp2(s - m)
        l = jnp.sum(p, axis=-1, keepdims=True)
        live[t] = (p.astype(BF16), l, m + jnp.log2(l) if need_lse else None)

    def values(t):
        p, l, lse2 = live.pop(t)
        v = units[t][2]()
        o2 = jnp.dot(p, v, preferred_element_type=F32) / l
        nq = o2.shape[0] // 2
        lse = None
        if need_lse:
            lse = jnp.where(lo, lse2[:nq], lse2[nq:])
        units[t][3](jnp.where(lo, o2[:nq], o2[nq:]), lse)

    for t in range(n + 2):
        if t < n:
            scores(t)
        if 1 <= t <= n:
            numerator(t - 1)
        if t >= 2:
            values(t - 2)


def _na_kernel(q_ref, k_ref, v_ref, g_ref, bias_ref, o_ref, *, rows, rq):
    rb = pl.program_id(2)
    lo = _lane_lo()
    kwin = NA_ROWS * GRID_W

    def unit(rr):
        r = rb * rq + rr
        rs = jnp.clip(r - NA_ROWS // 2, 0, rows - NA_ROWS)
        shift = rs - r + (NA_ROWS - 1)
        window = pl.ds(pl.multiple_of(rs * GRID_W, GRID_W), kwin)
        rows_q = slice(rr * GRID_W, (rr + 1) * GRID_W)

        def load_bias():
            return jnp.concatenate([bias_ref[shift + 2 * i] for i in range(NA_ROWS // 2)], axis=1)

        def finish(o, lse):
            o_ref[rows_q, :] = (o * g_ref[rows_q, :].astype(F32)).astype(BF16)

        return (lambda: (q_ref[rows_q, :], k_ref[window, :]), load_bias, lambda: v_ref[window, :], finish)

    _attend_pipelined([unit(rr) for rr in range(rq)], lo, need_lse=False)


def _neighbourhood(proj3, bias_tab, rq=64):
    batch, seq_len, _ = proj3.shape
    rows = seq_len // GRID_W
    tq = rq * GRID_W
    return pl.pallas_call(
        functools.partial(_na_kernel, rows=rows, rq=rq),
        grid=(batch, PAIRS, rows // rq),
        in_specs=[
            pl.BlockSpec((None, tq, LANES), lambda b, hp, rb: (b, rb, 0 * PAIRS + hp)),
            pl.BlockSpec((None, seq_len, LANES), lambda b, hp, rb: (b, 0, 1 * PAIRS + hp)),
            pl.BlockSpec((None, seq_len, LANES), lambda b, hp, rb: (b, 0, 2 * PAIRS + hp)),
            pl.BlockSpec((None, tq, LANES), lambda b, hp, rb: (b, rb, 3 * PAIRS + hp)),
            pl.BlockSpec((None, 2 * NA_ROWS - 2, 2 * GRID_W, 2 * GRID_W), lambda b, hp, rb: (hp, 0, 0, 0)),
        ],
        out_specs=pl.BlockSpec((None, tq, LANES), lambda b, hp, rb: (b, rb, hp)),
        out_shape=jax.ShapeDtypeStruct((batch, seq_len, COL), BF16),
        compiler_params=pltpu.CompilerParams(
            dimension_semantics=("parallel", "parallel", "arbitrary"), vmem_limit_bytes=VMEM_LIMIT),
        name="na_attn",
    )(proj3, proj3, proj3, proj3, bias_tab)


def _na_bias_table(rel_bias):
    heads, n_dr, n_dc = rel_bias.shape
    c = jnp.arange(GRID_W)
    select = (c[None, None, :] - c[None, :, None] + NA_COLS - 1 == jnp.arange(n_dc)[:, None, None]).astype(F32)
    toeplitz = jnp.einsum('hrd,dqk->hrqk', rel_bias.astype(F32), select, precision=lax.Precision.HIGHEST)
    cs = jnp.clip(c - NA_COLS // 2, 0, GRID_W - NA_COLS)
    ok = (c[None, :] >= cs[:, None]) & (c[None, :] < cs[:, None] + NA_COLS)
    toeplitz = jnp.where(ok[None, None], toeplitz * LOG2_E, NEG)
    pairs = jnp.concatenate([toeplitz[:, :n_dr - 1], toeplitz[:, 1:]], axis=-1)
    pairs = pairs.reshape(heads // 2, 2, n_dr - 1, GRID_W, 2 * GRID_W).transpose(0, 2, 1, 3, 4)
    return pairs.reshape(heads // 2, n_dr - 1, 2 * GRID_W, 2 * GRID_W)


def _band_bias_table():
    nk = TQ + 2 * HALO
    qpos = jnp.arange(2 * TQ)[:, None] % TQ
    col = jnp.arange(nk)[None, :]
    band = jnp.abs(col - HALO - qpos) <= HALO
    masks = [band & ((col >= HALO) | (not first)) & ((col < TQ + HALO) | (not last))
             for last in (False, True) for first in (False, True)]
    return jnp.where(jnp.stack(masks), 0.0, NEG).astype(F32)


def _band_bias_index(at_start, at_end):
    idx = 0
    if at_start is not None:
        idx = idx + at_start.astype(jnp.int32)
    if at_end is not None:
        idx = idx + 2 * at_end.astype(jnp.int32)
    return idx


def _band_dilated_kernel(q_ref, kp_ref, kc_ref, kn_ref, vp_ref, vc_ref, vn_ref, bias_ref, out_ref, lse_ref,
                         o_scr, *, dil, tiles):
    i = pl.program_id(1)
    lo = _lane_lo()
    ch = TQ // HALO
    n_units = BLOCKS // dil // ch
    first, last = i == 0, i == tiles - 1

    def rows(chunk, ph):
        return slice((chunk * dil + ph) * HALO, (chunk * dil + ph + 1) * HALO)

    def unit(ph, a, hp):
        sl = slice(hp * LANES, (hp + 1) * LANES)
        halo = slice(ph * HALO, (ph + 1) * HALO)
        bias_idx = _band_bias_index(first if a == 0 else None, last if a == n_units - 1 else None)

        def chunks(ref):
            return [ref[rows(c, ph), sl] for c in range(a * ch, (a + 1) * ch)]

        def window(prev_ref, cur_ref, next_ref):
            head = prev_ref[halo, sl] if a == 0 else cur_ref[rows(a * ch - 1, ph), sl]
            tail = next_ref[halo, sl] if a == n_units - 1 else cur_ref[rows((a + 1) * ch, ph), sl]
            return jnp.concatenate([head] + chunks(cur_ref) + [tail], axis=0)

        def load_qk():
            return jnp.concatenate(chunks(q_ref), axis=0), window(kp_ref, kc_ref, kn_ref)

        def finish(o, lse):
            tok = pl.ds(a * TQ * dil + ph, TQ, stride=dil)
            o_scr[hp, tok, :] = o
            lse_ref[hp, tok, :] = lse

        return load_qk, lambda: bias_ref[bias_idx], lambda: window(vp_ref, vc_ref, vn_ref), finish

    units = [unit(ph, a, hp) for ph in range(dil) for a in range(n_units) for hp in range(PAIRS)]
    _attend_pipelined(units, lo, need_lse=True)
    for hp in range(PAIRS):
        out_ref[:, hp * LANES:(hp + 1) * LANES] = o_scr[hp].astype(BF16)


def _band_dilated(proj3, bias_tab, group, dil):
    batch, seq_len, _ = proj3.shape
    tiles = seq_len // TILE
    cq, ck, cv = 4 + 3 * group, 5 + 3 * group, 6 + 3 * group
    halo_rows = dil * HALO
    per_tile = TILE // halo_rows
    last_unit = seq_len // halo_rows - 1

    def cur(col):
        return pl.BlockSpec((None, TILE, COL), lambda b, i: (b, i, col))

    def prev(col):
        return pl.BlockSpec((None, halo_rows, COL), lambda b, i: (b, jnp.maximum(i * per_tile - 1, 0), col))

    def nxt(col):
        return pl.BlockSpec((None, halo_rows, COL),
                            lambda b, i: (b, jnp.minimum((i + 1) * per_tile, last_unit), col))

    return pl.pallas_call(
        functools.partial(_band_dilated_kernel, dil=dil, tiles=tiles),
        grid=(batch, tiles),
        in_specs=[cur(cq), prev(ck), cur(ck), nxt(ck), prev(cv), cur(cv), nxt(cv),
                  pl.BlockSpec(bias_tab.shape, lambda b, i: (0, 0, 0))],
        out_specs=[pl.BlockSpec((None, TILE, COL), lambda b, i: (b, i, 0)),
                   pl.BlockSpec((None, PAIRS, TILE, LANES), lambda b, i: (b, 0, i, 0))],
        out_shape=[jax.ShapeDtypeStruct((batch, seq_len, COL), BF16),
                   jax.ShapeDtypeStruct((batch, PAIRS, seq_len, LANES), F32)],
        scratch_shapes=[pltpu.VMEM((PAIRS, TILE, LANES), F32)],
        compiler_params=pltpu.CompilerParams(
            dimension_semantics=("parallel", "arbitrary"), vmem_limit_bytes=VMEM_LIMIT),
        name="band_d%d" % dil,
    )(*([proj3] * 7), bias_tab)


def _band_merge_kernel(q_ref, kp_ref, kc_ref, kn_ref, vp_ref, vc_ref, vn_ref, bias_ref,
                       o1_ref, l1_ref, o2_ref, l2_ref, g_ref, out_ref, *, tm, tiles):
    i = pl.program_id(1)
    lo = _lane_lo()
    n_units = tm // TQ
    first, last = i == 0, i == tiles - 1

    def unit(a, hp):
        sl = slice(hp * LANES, (hp + 1) * LANES)
        qrows = slice(a * TQ, (a + 1) * TQ)
        bias_idx = _band_bias_index(first if a == 0 else None, last if a == n_units - 1 else None)

        def window(prev_ref, cur_ref, next_ref):
            if a == 0:
                return jnp.concatenate([prev_ref[:, sl], cur_ref[0:TQ + HALO, sl]], axis=0)
            if a == n_units - 1:
                return jnp.concatenate([cur_ref[tm - TQ - HALO:tm, sl], next_ref[:, sl]], axis=0)
            return cur_ref[a * TQ - HALO:(a + 1) * TQ + HALO, sl]

        def finish(o, lse):
            l1 = l1_ref[hp, qrows, :]
            l2 = l2_ref[hp, qrows, :]
            top = jnp.maximum(lse, jnp.maximum(l1, l2))
            w0 = jnp.exp2(lse - top)
            w1 = jnp.exp2(l1 - top)
            w2 = jnp.exp2(l2 - top)
            num = w0 * o + w1 * o1_ref[qrows, sl].astype(F32) + w2 * o2_ref[qrows, sl].astype(F32)
            ob = num / (w0 + w1 + w2)
            out_ref[qrows, sl] = (ob * g_ref[qrows, sl].astype(F32)).astype(BF16)

        return (lambda: (q_ref[qrows, sl], window(kp_ref, kc_ref, kn_ref)),
                lambda: bias_ref[bias_idx],
                lambda: window(vp_ref, vc_ref, vn_ref),
                finish)

    units = [unit(a, hp) for a in range(n_units) for hp in range(PAIRS)]
    _attend_pipelined(units, lo, need_lse=True)


def _band_merge(proj3, bias_tab, o1, l1, o2, l2, tm=2048):
    batch, seq_len, _ = proj3.shape
    tiles = seq_len // tm
    hb = tm // HALO
    last_halo = seq_len // HALO - 1
    cq, ck, cv, cg = 4, 5, 6, 13

    def cur(col):
        return pl.BlockSpec((None, tm, COL), lambda b, i: (b, i, col))

    def prev(col):
        return pl.BlockSpec((None, HALO, COL), lambda b, i: (b, jnp.maximum(i * hb - 1, 0), col))

    def nxt(col):
        return pl.BlockSpec((None, HALO, COL), lambda b, i: (b, jnp.minimum((i + 1) * hb, last_halo), col))

    tok_spec = pl.BlockSpec((None, tm, COL), lambda b, i: (b, i, 0))
    lse_spec = pl.BlockSpec((None, PAIRS, tm, LANES), lambda b, i: (b, 0, i, 0))
    return pl.pallas_call(
        functools.partial(_band_merge_kernel, tm=tm, tiles=tiles),
        grid=(batch, tiles),
        in_specs=[cur(cq), prev(ck), cur(ck), nxt(ck), prev(cv), cur(cv), nxt(cv),
                  pl.BlockSpec(bias_tab.shape, lambda b, i: (0, 0, 0)),
                  tok_spec, lse_spec, tok_spec, lse_spec, cur(cg)],
        out_specs=tok_spec,
        out_shape=jax.ShapeDtypeStruct((batch, seq_len, COL), BF16),
        compiler_params=pltpu.CompilerParams(
            dimension_semantics=("parallel", "arbitrary"), vmem_limit_bytes=VMEM_LIMIT),
        name="band_merge",
    )(*([proj3] * 7), bias_tab, o1, l1, o2, l2, proj3)


def _out_kernel(x_ref, ua_ref, ub_ref, sa_ref, sb_ref, wa_ref, wb_ref, wo_ref, y_ref):
    br_a = jnp.dot(ua_ref[...], wa_ref[...], preferred_element_type=F32)
    br_b = jnp.dot(ub_ref[...], wb_ref[...], preferred_element_type=F32)
    merged = sa_ref[...].astype(F32) * br_a + sb_ref[...].astype(F32) * br_b
    y_ref[...] = x_ref[...] + jnp.dot(merged.astype(BF16), wo_ref[...], preferred_element_type=F32)


def _output(x2d, ua, ub, proj, wa, wb, wo, tm=1024):
    n_tok = x2d.shape[0]
    sig_a = (14 * COL) // D_MODEL
    sig_b = (16 * COL) // D_MODEL
    return pl.pallas_call(
        _out_kernel,
        grid=(n_tok // tm,),
        in_specs=[
            pl.BlockSpec((tm, D_MODEL), lambda i: (i, 0)),
            pl.BlockSpec((tm, COL), lambda i: (i, 0)),
            pl.BlockSpec((tm, COL), lambda i: (i, 0)),
            pl.BlockSpec((tm, D_MODEL), lambda i: (i, sig_a)),
            pl.BlockSpec((tm, D_MODEL), lambda i: (i, sig_b)),
            pl.BlockSpec((COL, D_MODEL), lambda i: (0, 0)),
            pl.BlockSpec((COL, D_MODEL), lambda i: (0, 0)),
            pl.BlockSpec((D_MODEL, D_MODEL), lambda i: (0, 0)),
        ],
        out_specs=pl.BlockSpec((tm, D_MODEL), lambda i: (i, 0)),
        out_shape=jax.ShapeDtypeStruct((n_tok, D_MODEL), F32),
        compiler_params=pltpu.CompilerParams(
            dimension_semantics=("parallel",), vmem_limit_bytes=VMEM_LIMIT),
        name="out_proj",
    )(x2d, ua, ub, proj, proj, wa, wb, wo)


def _rope_tables(seq_len):
    inv = ROPE_THETA ** (-jnp.arange(0, HEAD_DIM, 2, dtype=F32) / HEAD_DIM)
    ang = jnp.arange(seq_len, dtype=F32)[:, None] * inv[None, :]
    half = HEAD_DIM // 2
    lane = jnp.arange(LANES)
    pick = (lane[None, :] % half == jnp.arange(half)[:, None]).astype(F32)
    sign = jnp.where(lane % HEAD_DIM < half, -1.0, 1.0).astype(F32)
    cos_t = jnp.dot(jnp.cos(ang), pick, precision=lax.Precision.HIGHEST)
    sin_t = jnp.dot(jnp.sin(ang), pick * sign[None, :], precision=lax.Precision.HIGHEST)
    return cos_t, sin_t


def _tile_gains(qn_a, kn_a, qn_b, kn_b):
    scale = LOG2_E / math.sqrt(HEAD_DIM)
    heads = COL // HEAD_DIM
    rows = []
    for t in range(N_COL_TILES):
        if t == 0:
            g = qn_a
        elif t == 1:
            g = kn_a
        elif t in (4, 7, 10):
            g = qn_b
        elif t in (5, 8, 11):
            g = kn_b
        else:
            g = jnp.ones((HEAD_DIM,), F32)
        g = g.astype(F32)
        if t in _Q_TILES:
            g = g * scale
        rows.append(jnp.tile(g, heads))
    return jnp.stack(rows)[:, None, :]


def _layer(x, ng, w_bf, gains, bias_tab, bd, cos, sin, wa, wb, wo):
    batch, seq_len, _ = x.shape
    assert seq_len % TILE == 0 and seq_len <= cos.shape[0]
    x2d = x.reshape(batch * seq_len, D_MODEL)
    proj = _project(x2d, seq_len, ng, w_bf, gains, cos, sin, bd)
    proj3 = proj.reshape(batch, seq_len, D_IN)
    ua = _neighbourhood(proj3, bias_tab)
    band_tab = _band_bias_table()
    o2, l2 = _band_dilated(proj3, band_tab, 2, DIL_GROUPS[2][1])
    o1, l1 = _band_dilated(proj3, band_tab, 1, DIL_GROUPS[1][1])
    ub = _band_merge(proj3, band_tab, o1, l1, o2, l2)
    y = _output(x2d, ua.reshape(-1, COL), ub.reshape(-1, COL), proj, wa, wb, wo)
    return y.reshape(batch, seq_len, D_MODEL)


def kernel(x_prompt, x_sample, norm_gain, w_in, qn_a, kn_a, rel_bias_a, qn_b, kn_b,
           w_branch_a, w_branch_b, w_out):
    depth = norm_gain.shape[0]
    blk = jnp.arange(256) // HEAD_DIM
    bd = jnp.where(blk[:, None] == blk[None, :], 1.0 / HEAD_DIM, 0.0).astype(BF16)
    cos, sin = _rope_tables(max(x_prompt.shape[1], x_sample.shape[1]))
    y_prompt, y_sample = x_prompt, x_sample
    for l in range(depth):
        ng = norm_gain[l].astype(F32)[None, :]
        w_bf = w_in[l].astype(BF16)
        gains = _tile_gains(qn_a[l], kn_a[l], qn_b[l], kn_b[l])
        bias_tab = _na_bias_table(rel_bias_a[l])
        wa = w_branch_a[l].astype(BF16)
        wb = w_branch_b[l].astype(BF16)
        wo = w_out[l].astype(BF16)
        y_prompt = _layer(y_prompt, ng, w_bf, gains, bias_tab, bd, cos, sin, wa, wb, wo)
        y_sample = _layer(y_sample, ng, w_bf, gains, bias_tab, bd, cos, sin, wa, wb, wo)
    return (y_prompt, y_sample)
```
